```python
import math
import jax
import jax.numpy as jnp
from jax import lax
import numpy as np

D_MODEL = 1024
BATCH = 32
SEQ = 2048
DEPTH = 2

GRID_W = 64
CTX_LEN = 256
HEAD_DIM = 64
N_MIXERS = 4
GROUP_WIDTH = D_MODEL // N_MIXERS

SWA_HEADS = GROUP_WIDTH // HEAD_DIM
SWA_KV_HEADS = 2
SWA_GQA = SWA_HEADS // SWA_KV_HEADS
SWA_WINDOW = 128
SWA_BLOCK = 128
ROPE_BASE = 10000.0

NA_HEADS = GROUP_WIDTH // HEAD_DIM
NA_ROWS = 8
NA_COLS = 16
NA_QB = 16
NA_KB = NA_QB + NA_COLS

HGRN_HEADS = GROUP_WIDTH // HEAD_DIM
HGRN_DK = HEAD_DIM
HGRN_DV = HEAD_DIM
HGRN_CHUNK = 64

S5_GROUP_CH = 16
S5_GROUPS = GROUP_WIDTH // S5_GROUP_CH
S5_STATE = 64

MOE_GROUPS = 4
MOE_PER_GROUP = 8
MOE_EXPERTS = MOE_GROUPS * MOE_PER_GROUP
MOE_TOPK = 2
MOE_HIDDEN = 512

NORM_EPS = 1e-6
NEG_INF = -1e30

PROJ_SIZES = (GROUP_WIDTH, SWA_KV_HEADS * HEAD_DIM, SWA_KV_HEADS * HEAD_DIM,
              GROUP_WIDTH, GROUP_WIDTH, GROUP_WIDTH,
              GROUP_WIDTH, GROUP_WIDTH, GROUP_WIDTH, GROUP_WIDTH, GROUP_WIDTH,
              GROUP_WIDTH)
PROJ_WIDTH = sum(PROJ_SIZES)

kernel_name = 'hybrid_dit_parallel_groups_hmoe'


def _rmsnorm(x, w):
    xf = x.astype(jnp.float32)
    y = xf * lax.rsqrt(jnp.mean(xf * xf, axis=-1, keepdims=True) + NORM_EPS)
    return (y * w.astype(jnp.float32)).astype(x.dtype)


def _modulate(h, shift, scale):
    return h * (1 + scale) + shift


def _rotate(x, ang):
    n = x.shape[-1] // 2
    cos = jnp.cos(ang)[None, :, None, :].astype(x.dtype)
    sin = jnp.sin(ang)[None, :, None, :].astype(x.dtype)
    x1, x2 = x[..., :n], x[..., n:]
    return jnp.concatenate([x1 * cos - x2 * sin, x1 * sin + x2 * cos], axis=-1)


def _rope_2d(x, row, col):
    half = HEAD_DIM // 2
    inv = 1.0 / (ROPE_BASE ** (jnp.arange(0, half, 2, dtype=jnp.float32) / half))
    ang_r = row.astype(jnp.float32)[:, None] * inv[None, :]
    ang_c = col.astype(jnp.float32)[:, None] * inv[None, :]
    return jnp.concatenate([_rotate(x[..., :half], ang_r), _rotate(x[..., half:], ang_c)], axis=-1)


def _sink_softmax(s, sink):
    m = jnp.maximum(jnp.max(s, axis=-1, keepdims=True), sink)
    e = jnp.exp(s - m)
    return e / (jnp.sum(e, axis=-1, keepdims=True) + jnp.exp(sink - m))


def _swa_mixer(q, k, v, qc, kc, vc, sink, row, col, need_ctx):
    bsz, n = q.shape[0], q.shape[1]
    scale = HEAD_DIM ** -0.5
    qg = (_rope_2d(q, row, col) * scale).reshape(bsz, n, SWA_KV_HEADS, SWA_GQA, HEAD_DIM)
    pad = ((0, 0), (SWA_WINDOW, SWA_WINDOW), (0, 0), (0, 0))
    kp = jnp.pad(_rope_2d(k, row, col), pad)
    vp = jnp.pad(v, pad)
    sink_b = sink.astype(jnp.float32).reshape(SWA_KV_HEADS, SWA_GQA, 1, 1)
    span = SWA_BLOCK + 2 * SWA_WINDOW

    def block(i):
        start = i * SWA_BLOCK
        qb = lax.dynamic_slice_in_dim(qg, start, SWA_BLOCK, axis=1)
        kb = lax.dynamic_slice_in_dim(kp, start, span, axis=1)
        vb = lax.dynamic_slice_in_dim(vp, start, span, axis=1)
        qpos = start + jnp.arange(SWA_BLOCK)
        kpos = start - SWA_WINDOW + jnp.arange(span)
        valid = ((jnp.abs(qpos[:, None] - kpos[None, :]) <= SWA_WINDOW)
                 & (kpos >= 0)[None, :] & (kpos < n)[None, :])
        s_lat = jnp.einsum('bqkgd,bskd->bkgqs', qb, kb).astype(jnp.float32)
        s_lat = jnp.where(valid, s_lat, NEG_INF)
        s_ctx = jnp.einsum('bqkgd,bskd->bkgqs', qb, kc).astype(jnp.float32)
        p = _sink_softmax(jnp.concatenate([s_lat, s_ctx], axis=-1), sink_b).astype(v.dtype)
        o = (jnp.einsum('bkgqs,bskd->bqkgd', p[..., :span], vb)
             + jnp.einsum('bkgqs,bskd->bqkgd', p[..., span:], vc))
        return o.reshape(bsz, SWA_BLOCK, SWA_HEADS * HEAD_DIM)

    out = lax.map(block, jnp.arange(n // SWA_BLOCK))
    out = jnp.moveaxis(out, 0, 1).reshape(bsz, n, SWA_HEADS * HEAD_DIM)
    out_c = None
    if need_ctx:
        lc = qc.shape[1]
        qcg = (qc * scale).reshape(bsz, lc, SWA_KV_HEADS, SWA_GQA, HEAD_DIM)
        s = jnp.einsum('bqkgd,bskd->bkgqs', qcg, kc).astype(jnp.float32)
        p = _sink_softmax(s, sink_b).astype(v.dtype)
        out_c = jnp.einsum('bkgqs,bskd->bqkgd', p, vc).reshape(bsz, lc, SWA_HEADS * HEAD_DIM)
    return out, out_c


def _na_mixer(q, k, v, qc, kc, vc, rpb, rows, need_ctx):
    bsz, n = q.shape[0], q.shape[1]
    wr = min(NA_ROWS, rows)
    ncb = GRID_W // NA_QB
    scale = HEAD_DIM ** -0.5
    qgrid = (q * scale).reshape(bsz, rows, GRID_W, NA_HEADS, HEAD_DIM)
    kgrid = k.reshape(bsz, rows, GRID_W, NA_HEADS, HEAD_DIM)
    vgrid = v.reshape(bsz, rows, GRID_W, NA_HEADS, HEAD_DIM)
    row_start = jnp.clip(jnp.arange(rows) - wr // 2, 0, rows - wr)
    qcols = np.arange(GRID_W).reshape(ncb, NA_QB)
    win_start = np.clip(qcols - NA_COLS // 2, 0, GRID_W - NA_COLS)
    kcols = (np.clip(np.arange(ncb) * NA_QB - NA_COLS // 2, 0, GRID_W - NA_KB)[:, None]
             + np.arange(NA_KB)[None, :])
    col_valid = ((kcols[:, None, :] >= win_start[:, :, None])
                 & (kcols[:, None, :] < win_start[:, :, None] + NA_COLS))
    col_idx = np.clip(kcols[:, None, :] - qcols[:, :, None] + NA_COLS - 1, 0, 2 * NA_COLS - 2)
    bias = jnp.transpose(rpb.astype(jnp.float32)[:, :, col_idx], (0, 2, 3, 1, 4))
    bias = jnp.where(col_valid[None, :, :, None, :], bias, NEG_INF)
    n_lat = wr * NA_KB

    def row_block(r):
        rs = row_start[r]
        qr = qgrid[:, r].reshape(bsz, ncb, NA_QB, NA_HEADS, HEAD_DIM)
        kb = lax.dynamic_slice_in_dim(kgrid, rs, wr, axis=1)[:, :, kcols]
        vb = lax.dynamic_slice_in_dim(vgrid, rs, wr, axis=1)[:, :, kcols]
        b_r = jnp.take(bias, rs + jnp.arange(wr) - r + NA_ROWS - 1, axis=3)
        s_lat = jnp.einsum('bjqhd,brjkhd->bhjqrk', qr, kb).astype(jnp.float32) + b_r[None]
        s_lat = s_lat.reshape(bsz, NA_HEADS, ncb, NA_QB, n_lat)
        s_ctx = jnp.einsum('bjqhd,bshd->bhjqs', qr, kc).astype(jnp.float32)
        p = jax.nn.softmax(jnp.concatenate([s_lat, s_ctx], axis=-1), axis=-1).astype(v.dtype)
        p_lat = p[..., :n_lat].reshape(bsz, NA_HEADS, ncb, NA_QB, wr, NA_KB)
        o = (jnp.einsum('bhjqrk,brjkhd->bjqhd', p_lat, vb)
             + jnp.einsum('bhjqs,bshd->bjqhd', p[..., n_lat:], vc))
        return o.reshape(bsz, GRID_W, NA_HEADS * HEAD_DIM)

    out = lax.map(row_block, jnp.arange(rows))
    out = jnp.moveaxis(out, 0, 1).reshape(bsz, n, NA_HEADS * HEAD_DIM)
    out_c = None
    if need_ctx:
        lc = qc.shape[1]
        s = jnp.einsum('bqhd,bshd->bhqs', qc * scale, kc).astype(jnp.float32)
        p = jax.nn.softmax(s, axis=-1).astype(v.dtype)
        out_c = jnp.einsum('bhqs,bshd->bqhd', p, vc).reshape(bsz, lc, NA_HEADS * HEAD_DIM)
    return out, out_c


def _hgrn_forget(f_raw, lb):
    f = lb + (1.0 - lb) * jax.nn.sigmoid(f_raw)
    return 1.0 - f, jnp.log(f)


def _gla_chunked(q, k, v, logf, s0):
    bsz, n, nh = q.shape[0], q.shape[1], q.shape[2]
    nc = n // HGRN_CHUNK

    def chunks(t):
        return jnp.transpose(t.reshape(bsz, nc, HGRN_CHUNK, nh, t.shape[-1]), (1, 0, 3, 2, 4))

    causal = jnp.tril(jnp.ones((HGRN_CHUNK, HGRN_CHUNK), dtype=bool))[:, :, None]

    def step(state, blk):
        qb, kb, vb, gb = blk
        b = jnp.cumsum(gb, axis=2)
        b_last = b[:, :, -1:, :]
        diff = jnp.where(causal, b[:, :, :, None, :] - b[:, :, None, :, :], -jnp.inf)
        attn = jnp.einsum('bhtk,bhsk,bhtsk->bhts', qb, kb, jnp.exp(diff))
        o = (jnp.einsum('bhtk,bhkv->bhtv', qb * jnp.exp(b), state)
             + jnp.einsum('bhts,bhsv->bhtv', attn, vb))
        state = (jnp.exp(b_last[:, :, 0, :])[..., None] * state
                 + jnp.einsum('bhsk,bhsv->bhkv', kb * jnp.exp(b_last - b), vb))
        return state, o

    state, o = lax.scan(step, s0, (chunks(q), chunks(k), chunks(v), chunks(logf)))
    o = jnp.transpose(o, (1, 0, 3, 2, 4)).reshape(bsz, n, nh, v.shape[-1])
    return o, state


def _gated_rmsnorm(o, g, w):
    o = o * lax.rsqrt(jnp.mean(o * o, axis=-1, keepdims=True) + NORM_EPS) * w.astype(jnp.float32)
    gate = jax.nn.silu(g.astype(jnp.float32)).reshape(o.shape)
    return (o * gate).reshape(o.shape[0], o.shape[1], -1)


def _flip(t, rev):
    return jnp.flip(t, axis=1) if rev else t


def _hgrn2_mixer(q, f_fw, f_bw, i, g, qc, fc_fw, fc_bw, ic, gc, lb_fw, lb_bw, norm_w, need_ctx):
    def heads(t):
        return t.astype(jnp.float32).reshape(t.shape[0], t.shape[1], HGRN_HEADS, -1)

    qh, vh = heads(jax.nn.silu(q)), heads(i)
    qch, vch = heads(jax.nn.silu(qc)), heads(ic)
    s0 = jnp.zeros((q.shape[0], HGRN_HEADS, HGRN_DK, HGRN_DV), jnp.float32)
    o = jnp.zeros_like(vh)
    oc = jnp.zeros_like(vch) if need_ctx else None
    for rev, f_raw, fc_raw, lb in ((False, f_fw, fc_fw, lb_fw), (True, f_bw, fc_bw, lb_bw)):
        lbh = lb.reshape(HGRN_HEADS, HGRN_DK)
        kh, logf = _hgrn_forget(heads(f_raw), lbh)
        kch, logfc = _hgrn_forget(heads(fc_raw), lbh)
        o_ctx, s_ctx = _gla_chunked(_flip(qch, rev), _flip(kch, rev), _flip(vch, rev), _flip(logfc, rev), s0)
        o_lat, _ = _gla_chunked(_flip(qh, rev), _flip(kh, rev), _flip(vh, rev), _flip(logf, rev), s_ctx)
        o = o + _flip(o_lat, rev)
        if need_ctx:
            oc = oc + _flip(o_ctx, rev)
    out = _gated_rmsnorm(o, g, norm_w).astype(q.dtype)
    out_c = _gated_rmsnorm(oc, gc, norm_w).astype(q.dtype) if need_ctx else None
    return out, out_c


def _s5_discretize(lam_re, lam_im, log_dt, b_re, b_im):
    dt = jnp.exp(log_dt)[:, None]
    mag = jnp.exp(lam_re * dt)
    a_re, a_im = mag * jnp.cos(lam_im * dt), mag * jnp.sin(lam_im * dt)
    den = lam_re * lam_re + lam_im * lam_im
    k_re = ((a_re - 1.0) * lam_re + a_im * lam_im) / den
    k_im = (a_im * lam_re - (a_re - 1.0) * lam_im) / den
    bb_re = k_re[..., None] * b_re - k_im[..., None] * b_im
    bb_im = k_re[..., None] * b_im + k_im[..., None] * b_re
    return a_re, a_im, bb_re, bb_im


def _s5_scan(u, a_re, a_im, bb_re, bb_im, h0=None):
    n = u.shape[1]
    x_re = jnp.einsum('blgh,gph->blgp', u, bb_re)
    x_im = jnp.einsum('blgh,gph->blgp', u, bb_im)
    if h0 is not None:
        h_re, h_im = h0
        x_re = x_re.at[:, 0].add(a_re * h_re - a_im * h_im)
        x_im = x_im.at[:, 0].add(a_re * h_im + a_im * h_re)
    shape = (1, n) + a_re.shape

    def combine(e1, e2):
        a1r, a1i, b1r, b1i = e1
        a2r, a2i, b2r, b2i = e2
        return (a2r * a1r - a2i * a1i, a2r * a1i + a2i * a1r,
                a2r * b1r - a2i * b1i + b2r, a2r * b1i + a2i * b1r + b2i)

    _, _, x_re, x_im = lax.associative_scan(
        combine, (jnp.broadcast_to(a_re, shape), jnp.broadcast_to(a_im, shape), x_re, x_im), axis=1)
    return x_re, x_im


def _s5_readout(x_re, x_im, c_re, c_im):
    y = jnp.einsum('blgp,ghp->blgh', x_re, c_re) - jnp.einsum('blgp,ghp->blgh', x_im, c_im)
    return y.reshape(y.shape[0], y.shape[1], -1)


def _s5_glu(y, w, b):
    z = jax.nn.gelu(y)
    return z * jax.nn.sigmoid(z @ w.astype(jnp.float32) + b.astype(jnp.float32))


def _s5_mixer(u, uc, lam_re, lam_im, log_dt, b_re, b_im, c_re, c_im, d, glu_w, glu_b, need_ctx):
    f32 = jnp.float32

    def groups(t):
        return t.astype(f32).reshape(t.shape[0], t.shape[1], S5_GROUPS, S5_GROUP_CH)

    ug, ucg = groups(u), groups(uc)
    y = d.astype(f32) * u.astype(f32)
    yc = d.astype(f32) * uc.astype(f32) if need_ctx else None
    for direc in range(2):
        rev = direc == 1
        a_re, a_im, bb_re, bb_im = _s5_discretize(
            lam_re[direc].astype(f32), lam_im[direc].astype(f32), log_dt[direc].astype(f32),
            b_re[direc].astype(f32), b_im[direc].astype(f32))
        cr, ci = c_re[direc].astype(f32), c_im[direc].astype(f32)
        xc_re, xc_im = _s5_scan(_flip(ucg, rev), a_re, a_im, bb_re, bb_im)
        x_re, x_im = _s5_scan(_flip(ug, rev), a_re, a_im, bb_re, bb_im, (xc_re[:, -1], xc_im[:, -1]))
        y = y + _flip(_s5_readout(x_re, x_im, cr, ci), rev)
        if need_ctx:
            yc = yc + _flip(_s5_readout(xc_re, xc_im, cr, ci), rev)
    out = _s5_glu(y, glu_w, glu_b).astype(u.dtype)
    out_c = _s5_glu(yc, glu_w, glu_b).astype(u.dtype) if need_ctx else None
    return out, out_c


def _token_mixers(h, hc, w_in, sink, rpb, lb_fw, lb_bw, hg_norm_w, s5_params, row, col, rows, need_ctx):
    cuts = [int(v) for v in np.cumsum(PROJ_SIZES)[:-1]]
    p = jnp.split(h @ w_in, cuts, axis=-1)
    pc = jnp.split(hc @ w_in, cuts, axis=-1)

    def hd(t):
        return t.reshape(t.shape[0], t.shape[1], -1, HEAD_DIM)

    a, a_c = _swa_mixer(hd(p[0]), hd(p[1]), hd(p[2]), hd(pc[0]), hd(pc[1]), hd(pc[2]),
                        sink, row, col, need_ctx)
    b, b_c = _na_mixer(hd(p[3]), hd(p[4]), hd(p[5]), hd(pc[3]), hd(pc[4]), hd(pc[5]),
                       rpb, rows, need_ctx)
    cm, c_c = _hgrn2_mixer(p[6], p[7], p[8], p[9], p[10], pc[6], pc[7], pc[8], pc[9], pc[10],
                           lb_fw, lb_bw, hg_norm_w, need_ctx)
    dm, d_c = _s5_mixer(p[11], pc[11], *s5_params, need_ctx)
    mix = jnp.concatenate([a, b, cm, dm], axis=-1)
    mix_c = jnp.concatenate([a_c, b_c, c_c, d_c], axis=-1) if need_ctx else None
    return mix, mix_c


def _hier_moe(t, gw, gb, ew, eb, w_gate, w_up, w_down):
    g_logits = (t @ gw + gb).astype(jnp.float32)
    g_prob = jax.nn.softmax(g_logits, axis=-1)
    g_idx = jnp.argmax(g_logits, axis=-1)
    g_w = jnp.take_along_axis(g_prob, g_idx[:, None], axis=-1)
    e_logits = (t @ ew + eb).astype(jnp.float32).reshape(-1, MOE_GROUPS, MOE_PER_GROUP)
    e_in = jnp.take_along_axis(e_logits, g_idx[:, None, None], axis=1)[:, 0]
    top_v, top_i = lax.top_k(e_in, MOE_TOPK)
    w = jax.nn.softmax(top_v, axis=-1) * g_w
    eid = g_idx[:, None] * MOE_PER_GROUP + top_i
    combine = jnp.sum(jax.nn.one_hot(eid, MOE_EXPERTS, dtype=jnp.float32) * w[..., None], axis=1).astype(t.dtype)
    y = jnp.zeros_like(t)
    for e in range(MOE_EXPERTS):
        hid = jax.nn.silu(t @ w_gate[e]) * (t @ w_up[e])
        y = y + combine[:, e:e + 1] * (hid @ w_down[e])
    return y


def setup_inputs(seed: int = 0) -> dict:
    key = jax.random.key(seed)
    ks = iter(jax.random.split(key, 40))
    f32 = jnp.float32

    def nrm(shape, scale):
        return jax.random.normal(next(ks), shape, f32) * scale

    D = D_MODEL
    G, P, Hs = S5_GROUPS, S5_STATE, S5_GROUP_CH
    lam_im_base = jnp.broadcast_to(math.pi * jnp.arange(P, dtype=f32), (DEPTH, 2, G, P))
    return {
        'x': nrm((BATCH, SEQ, D), 1.0),
        'c': nrm((BATCH, D), 1.0),
        'ctx': nrm((BATCH, CTX_LEN, D), 1.0),
        'c_ctx': nrm((D,), 1.0),
        'mod_w': nrm((DEPTH, D, 6 * D), 0.5 * D ** -0.5),
        'mod_b': nrm((DEPTH, 6 * D), 0.01),
        'norm1_w': 1.0 + nrm((DEPTH, D), 0.01),
        'norm2_w': 1.0 + nrm((DEPTH, D), 0.01),
        'w_in': nrm((DEPTH, D, PROJ_WIDTH), D ** -0.5),
        'w_out': nrm((DEPTH, D, D), D ** -0.5),
        'swa_sink': nrm((DEPTH, SWA_HEADS), 0.5),
        'na_rpb': nrm((DEPTH, NA_HEADS, 2 * NA_ROWS - 1, 2 * NA_COLS - 1), 0.1),
        'hgrn_lb': nrm((2, DEPTH, GROUP_WIDTH), 0.5),
        'hgrn_norm_w': 1.0 + nrm((DEPTH, HGRN_DV), 0.01),
        's5_lam_re': -0.5 + nrm((DEPTH, 2, G, P), 0.01),
        's5_lam_im': lam_im_base + nrm((DEPTH, 2, G, P), 0.01),
        's5_log_dt': jax.random.uniform(next(ks), (DEPTH, 2, G), f32, math.log(1e-3), math.log(1e-1)),
        's5_b_re': nrm((DEPTH, 2, G, P, Hs), (2 * Hs) ** -0.5),
        's5_b_im': nrm((DEPTH, 2, G, P, Hs), (2 * Hs) ** -0.5),
        's5_c_re': nrm((DEPTH, 2, G, Hs, P), P ** -0.5),
        's5_c_im': nrm((DEPTH, 2, G, Hs, P), P ** -0.5),
        's5_d': nrm((DEPTH, GROUP_WIDTH), 1.0),
        's5_glu_w': nrm((DEPTH, GROUP_WIDTH, GROUP_WIDTH), GROUP_WIDTH ** -0.5),
        's5_glu_b': nrm((DEPTH, GROUP_WIDTH), 0.01),
        'moe_group_w': nrm((DEPTH, D, MOE_GROUPS), D ** -0.5),
        'moe_group_b': nrm((DEPTH, MOE_GROUPS), 0.01),
        'moe_expert_w': nrm((DEPTH, D, MOE_EXPERTS), D ** -0.5),
        'moe_expert_b': nrm((DEPTH, MOE_EXPERTS), 0.01),
        'moe_w_gate': nrm((DEPTH, MOE_EXPERTS, D, MOE_HIDDEN), D ** -0.5),
        'moe_w_up': nrm((DEPTH, MOE_EXPERTS, D, MOE_HIDDEN), D ** -0.5),
        'moe_w_down': nrm((DEPTH, MOE_EXPERTS, MOE_HIDDEN, D), MOE_HIDDEN ** -0.5),
        'final_norm_w': 1.0 + nrm((D,), 0.01),
    }


def reference(x, c, ctx, c_ctx, mod_w, mod_b, norm1_w, norm2_w, w_in, w_out,
              swa_sink, na_rpb, hgrn_lb, hgrn_norm_w,
              s5_lam_re, s5_lam_im, s5_log_dt, s5_b_re, s5_b_im, s5_c_re, s5_c_im, s5_d, s5_glu_w, s5_glu_b,
              moe_group_w, moe_group_b, moe_expert_w, moe_expert_b, moe_w_gate, moe_w_up, moe_w_down,
              final_norm_w):
    bsz, n_tok, dm = x.shape
    rows = n_tok // GRID_W
    t = jnp.arange(n_tok)
    row, col = t // GRID_W, t % GRID_W
    lbp = jax.nn.softmax(hgrn_lb.astype(jnp.float32), axis=1)
    lower_bound = jnp.cumsum(lbp, axis=1) - lbp[:, :1]
    c_act = jax.nn.silu(c)
    cc_act = jax.nn.silu(c_ctx)
    n_lat = bsz * n_tok
    for l in range(DEPTH):
        last = l == DEPTH - 1
        mod = jnp.split(c_act @ mod_w[l] + mod_b[l], 6, axis=-1)
        modc = jnp.split(cc_act @ mod_w[l] + mod_b[l], 6, axis=-1)
        h = _modulate(_rmsnorm(x, norm1_w[l]), mod[0][:, None], mod[1][:, None])
        hc = _modulate(_rmsnorm(ctx, norm1_w[l]), modc[0], modc[1])
        s5_params = (s5_lam_re[l], s5_lam_im[l], s5_log_dt[l], s5_b_re[l], s5_b_im[l],
                     s5_c_re[l], s5_c_im[l], s5_d[l], s5_glu_w[l], s5_glu_b[l])
        mix, mix_c = _token_mixers(h, hc, w_in[l], swa_sink[l], na_rpb[l],
                                   lower_bound[0, l], lower_bound[1, l], hgrn_norm_w[l],
                                   s5_params, row, col, rows, not last)
        x = x + mod[2][:, None] * (mix @ w_out[l])
        moe_w = (moe_group_w[l], moe_group_b[l], moe_expert_w[l], moe_expert_b[l],
                 moe_w_gate[l], moe_w_up[l], moe_w_down[l])
        h2 = _modulate(_rmsnorm(x, norm2_w[l]), mod[3][:, None], mod[4][:, None])
        if last:
            x = x + mod[5][:, None] * _hier_moe(h2.reshape(-1, dm), *moe_w).reshape(x.shape)
        else:
            ctx = ctx + modc[2] * (mix_c @ w_out[l])
            h2c = _modulate(_rmsnorm(ctx, norm2_w[l]), modc[3], modc[4])
            y = _hier_moe(jnp.concatenate([h2.reshape(-1, dm), h2c.reshape(-1, dm)], axis=0), *moe_w)
            x = x + mod[5][:, None] * y[:n_lat].reshape(x.shape)
            ctx = ctx + modc[5] * y[n_lat:].reshape(ctx.shape)
    return _rmsnorm(x, final_norm_w)
```

```python
import functools
import math

import numpy as np
import jax
import jax.numpy as jnp
from jax import lax
from jax.experimental import pallas as pl
from jax.experimental.pallas import tpu as pltpu

F32 = jnp.float32
BF16 = jnp.bfloat16

GRID_W = 64
HEAD_DIM = 64
GROUP_WIDTH = 256
SWA_WINDOW = 128
ROPE_BASE = 10000.0
NA_ROWS = 8
NA_COLS = 16
HGRN_CHUNK = 64
HGRN_SUB = 16
S5_GROUP_CH = 16
S5_GROUPS = 16
S5_STATE = 64
S5_CHUNK = GROUP_WIDTH // S5_GROUP_CH
MOE_GROUPS = 4
MOE_PER_GROUP = 8
MOE_EXPERTS = 32
MOE_HIDDEN = 512
NORM_EPS = 1e-6
NEG_INF = -1e30
EXP_CLAMP = 80.0

V7X_VMEM_LIMIT_BYTES = 56 * 1024 * 1024
TOKEN_TILE = 256
MOE_TILE = 256


def _cparams(sem):
    return pltpu.CompilerParams(dimension_semantics=sem, vmem_limit_bytes=V7X_VMEM_LIMIT_BYTES)


def _dot(a, b):
    return jnp.dot(a, b, preferred_element_type=F32)


def _dot_nt(a, b):
    return lax.dot_general(a, b, (((1,), (1,)), ((), ())), preferred_element_type=F32)


def _dot_tn(a, b):
    return lax.dot_general(a, b, (((0,), (0,)), ((), ())), preferred_element_type=F32)


def _split(a):
    hi = a.astype(BF16)
    lo = (a - hi.astype(F32)).astype(BF16)
    return hi, lo


def _dot_f32(a, b):
    ah, al = _split(a)
    bh, bl = _split(b)
    return _dot(ah, bh) + _dot(al, bh) + _dot(ah, bl)


def _dot_exact_rhs(a, b_bf16):
    a0 = a.astype(BF16)
    r1 = a - a0.astype(F32)
    a1 = r1.astype(BF16)
    a2 = (r1 - a1.astype(F32)).astype(BF16)
    return _dot(a0, b_bf16) + _dot(a1, b_bf16) + _dot(a2, b_bf16)


def _sigmoid(x):
    return 1.0 / (1.0 + jnp.exp(-x))


def _silu(x):
    return x * _sigmoid(x)


def _mod_kernel(c_ref, w_ref, b_ref, o_ref):
    c = c_ref[...]
    o_ref[0] = _dot_f32(_silu(c), w_ref[0]) + b_ref[0]


def _mods(c_all, mod_w, mod_b):
    depth, d, d6 = mod_w.shape
    r = c_all.shape[0]
    bn = 1024
    return pl.pallas_call(
        _mod_kernel,
        grid=(depth, d6 // bn),
        in_specs=[pl.BlockSpec((r, d), lambda l, j: (0, 0)),
                  pl.BlockSpec((1, d, bn), lambda l, j: (l, 0, j)),
                  pl.BlockSpec((1, 1, bn), lambda l, j: (l, 0, j))],
        out_specs=pl.BlockSpec((1, r, bn), lambda l, j: (l, 0, j)),
        out_shape=jax.ShapeDtypeStruct((depth, r, d6), F32),
        compiler_params=_cparams(("arbitrary", "arbitrary")),
        name="mods",
    )(c_all, mod_w, mod_b.reshape(depth, 1, d6))


def _proj_kernel(x_ref, ms_ref, nw_ref, w_ref, cs_ref, sn_ref, qa_ref, qb_ref, hg_ref, u_ref):
    x = x_ref[0]
    var = jnp.mean(x * x, axis=-1, keepdims=True)
    y = x * lax.rsqrt(var + NORM_EPS) * nw_ref[...]
    shift = ms_ref[0, 0, 0:1, :]
    scale = ms_ref[0, 0, 1:2, :]
    h = (y * (1.0 + scale) + shift).astype(BF16)
    r = _dot(h, w_ref[:, 0:768])
    roped = r[:, 0:384] * cs_ref[...] + r[:, 384:768] * sn_ref[...]
    qa_ref[0, :, 0:384] = roped.astype(BF16)
    t = _dot(h, w_ref[:, 768:1664])
    qa_ref[0, :, 384:512] = t[:, 0:128].astype(BF16)
    qb_ref[0, :, 0:256] = (t[:, 128:384] * (HEAD_DIM ** -0.5)).astype(BF16)
    qb_ref[0, :, 256:768] = t[:, 384:896].astype(BF16)
    hg_ref[0] = _dot(h, w_ref[:, 1664:2944])
    u_ref[0] = _dot(h, w_ref[:, 2944:3200])


def _proj(x_all, msel, nw, w_ext, cs, sn, lc, tm):
    b, s, d = x_all.shape
    nt = s // tm
    seg = lambda j: jnp.where(j * tm >= lc, 1, 0)
    return pl.pallas_call(
        _proj_kernel,
        grid=(b, nt),
        in_specs=[pl.BlockSpec((1, tm, d), lambda i, j: (i, j, 0)),
                  pl.BlockSpec((1, 1, 6, d), lambda i, j: (i, seg(j), 0, 0)),
                  pl.BlockSpec((1, d), lambda i, j: (0, 0)),
                  pl.BlockSpec((d, 3200), lambda i, j: (0, 0)),
                  pl.BlockSpec((tm, 384), lambda i, j: (j, 0)),
                  pl.BlockSpec((tm, 384), lambda i, j: (j, 0))],
        out_specs=[pl.BlockSpec((1, tm, 512), lambda i, j: (i, j, 0)),
                   pl.BlockSpec((1, tm, 768), lambda i, j: (i, j, 0)),
                   pl.BlockSpec((1, tm, 1280), lambda i, j: (i, j, 0)),
                   pl.BlockSpec((1, tm, 256), lambda i, j: (i, j, 0))],
        out_shape=[jax.ShapeDtypeStruct((b, s, 512), BF16),
                   jax.ShapeDtypeStruct((b, s, 768), BF16),
                   jax.ShapeDtypeStruct((b, s, 1280), F32),
                   jax.ShapeDtypeStruct((b, s, 256), F32)],
        compiler_params=_cparams(("arbitrary", "arbitrary")),
        name="proj",
    )(x_all, msel, nw, w_ext, cs, sn)


def _attend(q, parts, sink):
    scores = []
    for k, _, mask in parts:
        s = _dot_nt(q, k)
        if mask is not None:
            s = s + mask
        scores.append(s)
    m = functools.reduce(jnp.maximum, [jnp.max(s, axis=-1, keepdims=True) for s in scores])
    if sink is not None:
        m = jnp.maximum(m, sink)
    den = jnp.zeros_like(m)
    out = None
    for s, (_, v, _) in zip(scores, parts):
        p = jnp.exp(s - m)
        den = den + jnp.sum(p, axis=-1, keepdims=True)
        o = _dot(p.astype(BF16), v)
        out = o if out is None else out + o
    if sink is not None:
        den = den + jnp.exp(sink - m)
    return out / den


def _swa_kernel(sink_ref, qa_ref, o_ref, *, lc, n, tq, nct):
    j = pl.program_id(1)
    win = tq + 2 * SWA_WINDOW
    kc = qa_ref[0, 0:lc, 256:384]
    vc = qa_ref[0, 0:lc, 384:512]

    def heads(q, lat):
        outs = []
        for h in range(4):
            kv = h // 2
            sl = slice(HEAD_DIM * kv, HEAD_DIM * (kv + 1))
            parts = []
            if lat is not None:
                kw, vw, mask = lat
                parts.append((kw[:, sl], vw[:, sl], mask))
            parts.append((kc[:, sl], vc[:, sl], None))
            outs.append(_attend(q[:, HEAD_DIM * h:HEAD_DIM * (h + 1)], parts, sink_ref[h]))
        return jnp.concatenate(outs, axis=-1).astype(BF16)

    if nct > 0:
        @pl.when(j < nct)
        def _():
            r0 = pl.multiple_of(j * tq, tq)
            o_ref[0] = heads(qa_ref[0, pl.ds(r0, tq), 0:256], None)

    @pl.when(j >= nct)
    def _():
        i = j - nct
        q0 = pl.multiple_of(lc + i * tq, 16)
        ks = jnp.clip(i * tq - SWA_WINDOW, 0, n - win)
        k0 = pl.multiple_of(lc + ks, 16)
        q = qa_ref[0, pl.ds(q0, tq), 0:256]
        kw = qa_ref[0, pl.ds(k0, win), 256:384]
        vw = qa_ref[0, pl.ds(k0, win), 384:512]
        qpos = i * tq + lax.broadcasted_iota(jnp.int32, (tq, win), 0)
        kpos = ks + lax.broadcasted_iota(jnp.int32, (tq, win), 1)
        mask = jnp.where(jnp.abs(qpos - kpos) <= SWA_WINDOW, 0.0, NEG_INF).astype(F32)
        o_ref[0] = heads(q, (kw, vw, mask))


def _swa(qa, sink, lc, need_ctx):
    b, s, _ = qa.shape
    n = s - lc
    tq = min(128, lc)
    nct = lc // tq if need_ctx else 0
    off = 0 if need_ctx else lc // tq
    return pl.pallas_call(
        functools.partial(_swa_kernel, lc=lc, n=n, tq=tq, nct=nct),
        grid=(b, nct + n // tq),
        in_specs=[pl.BlockSpec(memory_space=pltpu.SMEM),
                  pl.BlockSpec((1, s, 512), lambda i, j: (i, 0, 0))],
        out_specs=pl.BlockSpec((1, tq, 256), lambda i, j: (i, j + off, 0)),
        out_shape=jax.ShapeDtypeStruct((b, s, 256), BF16),
        compiler_params=_cparams(("arbitrary", "arbitrary")),
        name="swa",
    )(sink, qa)


def _na_kernel(qb_ref, bias_ref, o_ref, *, lc, rows, nct):
    j = pl.program_id(1)
    nk = NA_ROWS * GRID_W
    kc = qb_ref[0, 0:lc, 256:512]
    vc = qb_ref[0, 0:lc, 512:768]

    def heads(q, lat):
        outs = []
        for h in range(4):
            sl = slice(HEAD_DIM * h, HEAD_DIM * (h + 1))
            parts = []
            if lat is not None:
                kw, vw = lat
                parts.append((kw[:, sl], vw[:, sl], bias_ref[0, h]))
            parts.append((kc[:, sl], vc[:, sl], None))
            outs.append(_attend(q[:, sl], parts, None))
        return jnp.concatenate(outs, axis=-1).astype(BF16)

    if nct > 0:
        @pl.when(j < nct)
        def _():
            r0 = pl.multiple_of(j * GRID_W, GRID_W)
            o_ref[0] = heads(qb_ref[0, pl.ds(r0, GRID_W), 0:256], None)

    @pl.when(j >= nct)
    def _():
        r = j - nct
        rs = jnp.clip(r - NA_ROWS // 2, 0, rows - NA_ROWS)
        q0 = pl.multiple_of(lc + r * GRID_W, GRID_W)
        k0 = pl.multiple_of(lc + rs * GRID_W, GRID_W)
        q = qb_ref[0, pl.ds(q0, GRID_W), 0:256]
        kw = qb_ref[0, pl.ds(k0, nk), 256:512]
        vw = qb_ref[0, pl.ds(k0, nk), 512:768]
        o_ref[0] = heads(q, (kw, vw))


def _na(qb, bias, lc, need_ctx):
    b, s, _ = qb.shape
    n = s - lc
    rows = n // GRID_W
    nct = lc // GRID_W if need_ctx else 0
    off = 0 if need_ctx else lc // GRID_W

    def cls(j):
        r = jnp.maximum(j - nct, 0)
        return r - jnp.clip(r - NA_ROWS // 2, 0, rows - NA_ROWS)

    return pl.pallas_call(
        functools.partial(_na_kernel, lc=lc, rows=rows, nct=nct),
        grid=(b, nct + rows),
        in_specs=[pl.BlockSpec((1, s, 768), lambda i, j: (i, 0, 0)),
                  pl.BlockSpec((1, 4, GRID_W, NA_ROWS * GRID_W), lambda i, j: (cls(j), 0, 0, 0))],
        out_specs=pl.BlockSpec((1, GRID_W, 256), lambda i, j: (i, j + off, 0)),
        out_shape=jax.ShapeDtypeStruct((b, s, 256), BF16),
        compiler_params=_cparams(("arbitrary", "arbitrary")),
        name="na",
    )(qb, bias)


def _na_bias(rpb):
    d = np.arange(NA_ROWS)[:, None, None, None]
    c = np.arange(GRID_W)[None, :, None, None]
    jr = np.arange(NA_ROWS)[None, None, :, None]
    kc = np.arange(GRID_W)[None, None, None, :]
    ri = np.broadcast_to(jr - d + NA_ROWS - 1, (NA_ROWS, GRID_W, NA_ROWS, GRID_W))
    ci = np.broadcast_to(np.clip(kc - c + NA_COLS - 1, 0, 2 * NA_COLS - 2), ri.shape)
    ws = np.clip(c - NA_COLS // 2, 0, GRID_W - NA_COLS)
    valid = np.broadcast_to((kc >= ws) & (kc < ws + NA_COLS), ri.shape)
    t = rpb.astype(F32)[:, ri, ci]
    t = jnp.where(valid[None], t, NEG_INF)
    return jnp.transpose(t, (1, 0, 2, 3, 4)).reshape(NA_ROWS, 4, GRID_W, NA_ROWS * GRID_W)


def _hgrn_dir(blk, lb, st_ref, rev):
    c = HGRN_CHUNK
    sub = HGRN_SUB
    nsub = c // sub
    q = _silu(blk[:, 0:256])
    fraw = blk[:, 512:768] if rev else blk[:, 256:512]
    v = blk[:, 768:1024]
    f = lb + (1.0 - lb) * _sigmoid(fraw)
    kk = 1.0 - f
    logf = jnp.log(f)
    ti = lax.broadcasted_iota(jnp.int32, (c, c), 0)
    si = lax.broadcasted_iota(jnp.int32, (c, c), 1)
    tri = jnp.where((si >= ti) if rev else (si <= ti), 1.0, 0.0).astype(BF16)
    bcum = _dot_exact_rhs_left(tri, logf)
    btot = bcum[0:1] if rev else bcum[c - 1:c]
    vb = v.astype(BF16)
    st = st_ref[...]
    o_inter = _dot_nt((q * jnp.exp(bcum)).astype(BF16), st.astype(BF16))

    rh = lax.broadcasted_iota(jnp.int32, (c, 256), 0) // sub
    lh = lax.broadcasted_iota(jnp.int32, (c, 256), 1) // HEAD_DIM
    bd = jnp.where(rh == lh, 1.0, 0.0).astype(F32)
    outs = []
    for i in range(nsub):
        r0, r1 = i * sub, (i + 1) * sub
        if rev:
            rho = bcum[r1:r1 + 1] if i < nsub - 1 else jnp.zeros((1, 256), F32)
            k0, k1 = r0, c
        else:
            rho = bcum[r0 - 1:r0] if i > 0 else jnp.zeros((1, 256), F32)
            k0, k1 = 0, r1
        qh = q[r0:r1] * jnp.exp(bcum[r0:r1] - rho)
        kh = kk[k0:k1] * jnp.exp(jnp.minimum(rho - bcum[k0:k1], EXP_CLAMP))
        qbd = (jnp.concatenate([qh] * 4, axis=0) * bd).astype(BF16)
        a = _dot_nt(qbd, kh.astype(BF16))
        nk = k1 - k0
        tq = r0 + lax.broadcasted_iota(jnp.int32, (c, nk), 0) % sub
        sk = k0 + lax.broadcasted_iota(jnp.int32, (c, nk), 1)
        a = jnp.where((sk >= tq) if rev else (sk <= tq), a, 0.0)
        o4 = _dot(a.astype(BF16), vb[k0:k1]) * bd
        outs.append(o4[0:sub] + o4[sub:2 * sub] + o4[2 * sub:3 * sub] + o4[3 * sub:4 * sub])
    o = o_inter + jnp.concatenate(outs, axis=0)

    kend = (kk * jnp.exp(btot - bcum)).astype(BF16)
    r2 = lax.broadcasted_iota(jnp.int32, (256, 256), 0) // HEAD_DIM
    l2 = lax.broadcasted_iota(jnp.int32, (256, 256), 1) // HEAD_DIM
    upd = jnp.where(r2 == l2, _dot_tn(vb, kend), 0.0)
    st_ref[...] = st * jnp.exp(btot) + upd
    return o


def _dot_exact_rhs_left(tri_bf16, a):
    a0 = a.astype(BF16)
    r1 = a - a0.astype(F32)
    a1 = r1.astype(BF16)
    a2 = (r1 - a1.astype(F32)).astype(BF16)
    return _dot(tri_bf16, a0) + _dot(tri_bf16, a1) + _dot(tri_bf16, a2)


def _hgrn_kernel(hf_ref, hb_ref, lb_ref, of_ref, ob_ref, stf_ref, stb_ref):
    @pl.when(pl.program_id(1) == 0)
    def _():
        stf_ref[...] = jnp.zeros_like(stf_ref)
        stb_ref[...] = jnp.zeros_like(stb_ref)

    of_ref[0] = _hgrn_dir(hf_ref[0], lb_ref[0:1, :], stf_ref, False)
    ob_ref[0] = _hgrn_dir(hb_ref[0], lb_ref[1:2, :], stb_ref, True)


def _hgrn(hg, lb2, lc):
    b, s, _ = hg.shape
    c = HGRN_CHUNK
    nc = s // c
    ncc = lc // c

    def bwd(j):
        return jnp.where(j < ncc, ncc - 1 - j, ncc + nc - 1 - j)

    return pl.pallas_call(
        _hgrn_kernel,
        grid=(b, nc),
        in_specs=[pl.BlockSpec((1, c, 1280), lambda i, j: (i, j, 0)),
                  pl.BlockSpec((1, c, 1280), lambda i, j: (i, bwd(j), 0)),
                  pl.BlockSpec((2, 256), lambda i, j: (0, 0))],
        out_specs=[pl.BlockSpec((1, c, 256), lambda i, j: (i, j, 0)),
                   pl.BlockSpec((1, c, 256), lambda i, j: (i, bwd(j), 0))],
        out_shape=[jax.ShapeDtypeStruct((b, s, 256), F32),
                   jax.ShapeDtypeStruct((b, s, 256), F32)],
        scratch_shapes=[pltpu.VMEM((256, 256), F32), pltpu.VMEM((256, 256), F32)],
        compiler_params=_cparams(("arbitrary", "arbitrary")),
        name="hgrn",
    )(hg, hg, lb2)


def _s5_kernel(u_ref, k_ref, bm_ref, cm_ref, at_ref, d_ref, y_ref,
               hin_ref, hst_ref, *, bsz, ncc, nc):
    u = u_ref[0]
    ub = u.astype(BF16)
    for m in range(4):
        hin_ref[m] = _dot(ub, bm_ref[0, m])

    def run(direction, order_fn, count, h0):
        a_re = at_ref[0, 2 * direction:2 * direction + 1, :]
        a_im = at_ref[0, 2 * direction + 1:2 * direction + 2, :]

        def body(t, carry):
            h_re, h_im = carry
            r0 = pl.multiple_of(order_fn(t) * bsz, 8)
            hst_ref[2 * direction, pl.ds(r0, bsz), :] = h_re
            hst_ref[2 * direction + 1, pl.ds(r0, bsz), :] = h_im
            n_re = a_re * h_re - a_im * h_im + hin_ref[2 * direction, pl.ds(r0, bsz), :]
            n_im = a_re * h_im + a_im * h_re + hin_ref[2 * direction + 1, pl.ds(r0, bsz), :]
            return n_re, n_im

        return lax.fori_loop(0, count, body, h0)

    z = jnp.zeros((bsz, S5_STATE), F32)
    run(0, lambda t: t, nc, (z, z))
    hb = run(1, lambda t: ncc - 1 - t, ncc, (z, z))
    run(1, lambda t: nc - 1 - t, nc - ncc, hb)

    y = _dot(ub, k_ref[0]) + d_ref[0] * u
    for m in range(4):
        y = y + _dot(hst_ref[m].astype(BF16), cm_ref[0, m])
    y_ref[0] = y


def _s5(ug, kmat, bmat, cmat, at, dvec, bsz, ncc):
    g, rows, _ = ug.shape
    nc = rows // bsz
    return pl.pallas_call(
        functools.partial(_s5_kernel, bsz=bsz, ncc=ncc, nc=nc),
        grid=(g,),
        in_specs=[pl.BlockSpec((1, rows, 256), lambda i: (i, 0, 0)),
                  pl.BlockSpec((1, 256, 256), lambda i: (i, 0, 0)),
                  pl.BlockSpec((1, 4, 256, S5_STATE), lambda i: (i, 0, 0, 0)),
                  pl.BlockSpec((1, 4, S5_STATE, 256), lambda i: (i, 0, 0, 0)),
                  pl.BlockSpec((1, 4, S5_STATE), lambda i: (i, 0, 0)),
                  pl.BlockSpec((1, 1, 256), lambda i: (i, 0, 0))],
        out_specs=pl.BlockSpec((1, rows, 256), lambda i: (i, 0, 0)),
        out_shape=jax.ShapeDtypeStruct((g, rows, 256), F32),
        scratch_shapes=[pltpu.VMEM((4, rows, S5_STATE), F32), pltpu.VMEM((4, rows, S5_STATE), F32)],
        compiler_params=_cparams(("arbitrary",)),
        name="s5",
    )(ug, kmat, bmat, cmat, at, dvec)


def _s5_matrices(lam_re, lam_im, log_dt, b_re, b_im, c_re, c_im, d):
    t = S5_CHUNK
    hp = lax.Precision.HIGHEST
    lam_re, lam_im, log_dt = lam_re.astype(F32), lam_im.astype(F32), log_dt.astype(F32)
    b_re, b_im, c_re, c_im = b_re.astype(F32), b_im.astype(F32), c_re.astype(F32), c_im.astype(F32)
    dt = jnp.exp(log_dt)[..., None]
    jj = jnp.arange(t + 1, dtype=F32)[:, None, None, None]
    mag = jnp.exp(lam_re * dt * jj)
    ang = lam_im * dt * jj
    aj_re, aj_im = mag * jnp.cos(ang), mag * jnp.sin(ang)
    a_re, a_im = aj_re[1], aj_im[1]
    den = lam_re * lam_re + lam_im * lam_im
    k_re = ((a_re - 1.0) * lam_re + a_im * lam_im) / den
    k_im = (a_im * lam_re - (a_re - 1.0) * lam_im) / den
    bb_re = k_re[..., None] * b_re - k_im[..., None] * b_im
    bb_im = k_re[..., None] * b_im + k_im[..., None] * b_re
    ab_re = aj_re[..., None] * bb_re - aj_im[..., None] * bb_im
    ab_im = aj_re[..., None] * bb_im + aj_im[..., None] * bb_re
    kj = (jnp.einsum('dghp,jdgpi->jdghi', c_re, ab_re, precision=hp)
          - jnp.einsum('dghp,jdgpi->jdghi', c_im, ab_im, precision=hp))
    s_i = np.arange(t)[:, None]
    t_i = np.arange(t)[None, :]
    kf = jnp.where((t_i >= s_i)[:, :, None, None, None], kj[np.clip(t_i - s_i, 0, t), 0], 0.0)
    kb = jnp.where((s_i >= t_i)[:, :, None, None, None], kj[np.clip(s_i - t_i, 0, t), 1], 0.0)
    kmat = jnp.transpose(kf + kb, (2, 0, 4, 1, 3)).reshape(S5_GROUPS, 256, 256)

    def state_in(ab, idx, direction):
        return jnp.transpose(ab[idx, direction], (1, 0, 3, 2)).reshape(S5_GROUPS, 256, S5_STATE)

    fw_idx = np.arange(t - 1, -1, -1)
    bw_idx = np.arange(t)
    bmat = jnp.stack([state_in(ab_re, fw_idx, 0), state_in(ab_im, fw_idx, 0),
                      state_in(ab_re, bw_idx, 1), state_in(ab_im, bw_idx, 1)], axis=1)

    def state_out(idx, direction):
        ar = aj_re[idx, direction][:, :, None, :]
        ai = aj_im[idx, direction][:, :, None, :]
        cr, ci = c_re[direction][None], c_im[direction][None]
        re = cr * ar - ci * ai
        im = -(cr * ai + ci * ar)
        f = lambda m: jnp.transpose(m, (1, 3, 0, 2)).reshape(S5_GROUPS, S5_STATE, 256)
        return f(re), f(im)

    cf_re, cf_im = state_out(np.arange(1, t + 1), 0)
    cb_re, cb_im = state_out(np.arange(t, 0, -1), 1)
    cmat = jnp.stack([cf_re, cf_im, cb_re, cb_im], axis=1)
    at = jnp.stack([aj_re[t, 0], aj_im[t, 0], aj_re[t, 1], aj_im[t, 1]], axis=1)
    dvec = jnp.tile(d.astype(F32).reshape(S5_GROUPS, 1, S5_GROUP_CH), (1, t, 1)).reshape(S5_GROUPS, 1, 256)
    return kmat.astype(BF16), bmat.astype(BF16), cmat.astype(BF16), at, dvec


def _gelu_tanh(x):
    return 0.5 * x * (1.0 + jnp.tanh(math.sqrt(2.0 / math.pi) * (x + 0.044715 * (x * x * x))))


def _outproj_kernel(a_ref, b_ref, of_ref, ob_ref, g_ref, y5_ref, x_ref, ms_ref, hnw_ref,
                    gw_ref, gb_ref, wo_ref, n2_ref, rw_ref, rb_ref,
                    xo_ref, h2_ref, rt_ref):
    o = of_ref[0] + ob_ref[0]
    r2 = lax.broadcasted_iota(jnp.int32, (256, 256), 0) // HEAD_DIM
    l2 = lax.broadcasted_iota(jnp.int32, (256, 256), 1) // HEAD_DIM
    hm = jnp.where(r2 == l2, 1.0, 0.0).astype(BF16)
    sq = o * o
    sh, sl = _split(sq)
    ms = (_dot(sh, hm) + _dot(sl, hm)) * (1.0 / HEAD_DIM)
    cm = o * lax.rsqrt(ms + NORM_EPS) * hnw_ref[...] * _silu(g_ref[0])
    z = _gelu_tanh(y5_ref[0])
    dm = z * _sigmoid(_dot(z.astype(BF16), gw_ref[...]) + gb_ref[...])
    mix = (_dot(a_ref[0], wo_ref[0:256, :]) + _dot(b_ref[0], wo_ref[256:512, :])
           + _dot(cm.astype(BF16), wo_ref[512:768, :]) + _dot(dm.astype(BF16), wo_ref[768:1024, :]))
    x = x_ref[0] + ms_ref[0, 0, 2:3, :] * mix
    xo_ref[0] = x
    var = jnp.mean(x * x, axis=-1, keepdims=True)
    h2 = x * lax.rsqrt(var + NORM_EPS) * n2_ref[...]
    h2 = h2 * (1.0 + ms_ref[0, 0, 4:5, :]) + ms_ref[0, 0, 3:4, :]
    h2_ref[0] = h2
    lg = _dot_f32(h2, rw_ref[...]) + rb_ref[...]
    tm = lg.shape[0]
    lane = lax.broadcasted_iota(jnp.int32, (tm, 128), 1)
    big = jnp.int32(1 << 20)
    ninf = jnp.float32(-jnp.inf)
    gmask = lane < MOE_GROUPS
    gl = jnp.where(gmask, lg, ninf)
    gmax = jnp.max(gl, axis=-1, keepdims=True)
    gidx = jnp.min(jnp.where(gl == gmax, lane, big), axis=-1, keepdims=True)
    gsum = jnp.sum(jnp.where(gmask, jnp.exp(gl - gmax), 0.0), axis=-1, keepdims=True)
    gwt = 1.0 / gsum
    e = lane - MOE_GROUPS
    emask = (e >= 0) & (e < MOE_EXPERTS) & (lax.shift_right_arithmetic(e, 3) == gidx)
    el = jnp.where(emask, lg, ninf)
    v1 = jnp.max(el, axis=-1, keepdims=True)
    i1 = jnp.min(jnp.where(el == v1, lane, big), axis=-1, keepdims=True)
    el2 = jnp.where(lane == i1, ninf, el)
    v2 = jnp.max(el2, axis=-1, keepdims=True)
    i2 = jnp.min(jnp.where(el2 == v2, lane, big), axis=-1, keepdims=True)
    t2 = jnp.exp(v2 - v1)
    w1 = gwt / (1.0 + t2)
    w2 = w1 * t2
    rt = jnp.where(lane == 0, (i1 - MOE_GROUPS).astype(F32),
                   jnp.where(lane == 1, (i2 - MOE_GROUPS).astype(F32),
                             jnp.where(lane == 2, w1, jnp.where(lane == 3, w2, 0.0))))
    rt_ref[0] = rt


def _outproj(a_out, b_out, o_f, o_b, hg, y5, x_all, msel, hnw, glu_w, glu_b, w_out, n2w, rw, rb,
             lc, tm, need_ctx):
    b, s, d = x_all.shape
    off = 0 if need_ctx else lc // tm
    nt = s // tm - off
    seg = lambda j: jnp.where((j + off) * tm >= lc, 1, 0)
    tok = lambda w: pl.BlockSpec((1, tm, w), lambda i, j: (i, j + off, 0))
    full = lambda r, c: pl.BlockSpec((r, c), lambda i, j: (0, 0))
    return pl.pallas_call(
        _outproj_kernel,
        grid=(b, nt),
        in_specs=[tok(256), tok(256), tok(256), tok(256),
                  pl.BlockSpec((1, tm, 256), lambda i, j: (i, j + off, 4)),
                  tok(256), tok(d),
                  pl.BlockSpec((1, 1, 6, d), lambda i, j: (i, seg(j), 0, 0)),
                  full(1, 256), full(256, 256), full(1, 256), full(d, d), full(1, d),
                  full(d, 128), full(1, 128)],
        out_specs=[tok(d), tok(d), tok(128)],
        out_shape=[jax.ShapeDtypeStruct((b, s, d), F32),
                   jax.ShapeDtypeStruct((b, s, d), F32),
                   jax.ShapeDtypeStruct((b, s, 128), F32)],
        compiler_params=_cparams(("arbitrary", "arbitrary")),
        name="outproj",
    )(a_out, b_out, o_f, o_b, hg, y5, x_all, msel, hnw, glu_w, glu_b, w_out, n2w, rw, rb)


def _moe_kernel(te_ref, nact_ref, src_ref, nsrc_ref, dst_ref, h2_hbm, wg_ref, wu_ref, wd_ref, y2_hbm,
                xbuf, obuf, gsem, ssem, *, tm):
    i = pl.program_id(0)
    nact = nact_ref[0]
    slot = lax.rem(i, 2)

    def gather_copy(row, sl, r):
        return pltpu.make_async_copy(h2_hbm.at[pl.ds(row, 1)], xbuf.at[sl, pl.ds(r, 1)], gsem.at[sl])

    def scatter_copy(row, sl, r):
        return pltpu.make_async_copy(obuf.at[sl, pl.ds(r, 1)], y2_hbm.at[pl.ds(row, 1)], ssem.at[sl])

    def for_rows(fn):
        def body(r, c):
            fn(r)
            return c
        lax.fori_loop(0, tm, body, 0)

    @pl.when(i == 0)
    def _():
        for_rows(lambda r: gather_copy(src_ref[0, 0, r], 0, r).start())

    @pl.when(i + 1 < nact)
    def _():
        for_rows(lambda r: gather_copy(nsrc_ref[0, 0, r], 1 - slot, r).start())

    @pl.when(i < nact)
    def _():
        for_rows(lambda r: gather_copy(0, slot, r).wait())

        @pl.when(i >= 2)
        def _():
            for_rows(lambda r: scatter_copy(0, slot, r).wait())

        x = xbuf[slot].astype(BF16)
        hid = _silu(_dot(x, wg_ref[0])) * _dot(x, wu_ref[0])
        obuf[slot] = _dot(hid.astype(BF16), wd_ref[0])
        for_rows(lambda r: scatter_copy(dst_ref[0, 0, r], slot, r).start())

        @pl.when(i == nact - 1)
        def _():
            for_rows(lambda r: scatter_copy(0, slot, r).wait())

            @pl.when(i >= 1)
            def _():
                for_rows(lambda r: scatter_copy(0, 1 - slot, r).wait())


def _moe(h2_flat, wg, wu, wd, tile_expert, row_src, row_dst, nact, n_out_rows, tm):
    d = h2_flat.shape[1]
    hid = wg.shape[2]
    nt = tile_expert.shape[0]
    src3 = row_src.reshape(nt, 1, tm)
    dst3 = row_dst.reshape(nt, 1, tm)
    smem_tile = lambda fn: pl.BlockSpec((1, 1, tm), fn, memory_space=pltpu.SMEM)
    return pl.pallas_call(
        functools.partial(_moe_kernel, tm=tm),
        grid_spec=pltpu.PrefetchScalarGridSpec(
            num_scalar_prefetch=2,
            grid=(nt,),
            in_specs=[smem_tile(lambda i, te, na: (i, 0, 0)),
                      smem_tile(lambda i, te, na: (jnp.minimum(i + 1, nt - 1), 0, 0)),
                      smem_tile(lambda i, te, na: (i, 0, 0)),
                      pl.BlockSpec(memory_space=pl.ANY),
                      pl.BlockSpec((1, d, hid), lambda i, te, na: (te[i], 0, 0)),
                      pl.BlockSpec((1, d, hid), lambda i, te, na: (te[i], 0, 0)),
                      pl.BlockSpec((1, hid, d), lambda i, te, na: (te[i], 0, 0))],
            out_specs=pl.BlockSpec(memory_space=pl.ANY),
            scratch_shapes=[pltpu.VMEM((2, tm, d), F32), pltpu.VMEM((2, tm, d), F32),
                            pltpu.SemaphoreType.DMA((2,)), pltpu.SemaphoreType.DMA((2,))]),
        out_shape=jax.ShapeDtypeStruct((n_out_rows, d), F32),
        compiler_params=_cparams(("arbitrary",)),
        name="moe",
    )(tile_expert, nact, src3, src3, dst3, h2_flat, wg, wu, wd)


def _dispatch_plan(eid, rows, n_rows_total, tm):
    t = eid.shape[0]
    p = 2 * t
    e_flat = jnp.concatenate([eid[:, 0], eid[:, 1]])
    src = jnp.concatenate([rows, rows])
    dst = jnp.concatenate([rows, rows + n_rows_total])
    onehot = (e_flat[:, None] == jnp.arange(MOE_EXPERTS)[None, :]).astype(jnp.int32)
    counts = jnp.sum(onehot, axis=0)
    padded = ((counts + tm - 1) // tm) * tm
    pad_end = jnp.cumsum(padded)
    pad_start = pad_end - padded
    rank = jnp.sum((jnp.cumsum(onehot, axis=0) - onehot) * onehot, axis=1)
    slot = pad_start[e_flat] + rank
    nt = p // tm + MOE_EXPERTS
    nslots = nt * tm
    trash = 2 * n_rows_total + (jnp.arange(nslots, dtype=jnp.int32) % (2 * tm))
    row_src = jnp.zeros((nslots,), jnp.int32).at[slot].set(src)
    row_dst = trash.at[slot].set(dst)
    nact = (pad_end[-1] // tm).astype(jnp.int32)
    tile_start = jnp.arange(nt, dtype=jnp.int32) * tm
    te = jnp.searchsorted(pad_end, tile_start, side='right').astype(jnp.int32)
    te_last = te[jnp.maximum(nact - 1, 0)]
    te = jnp.where(tile_start < pad_end[-1], te, te_last)
    return te, row_src, row_dst, nact.reshape(1)


def _combine_kernel(x_ref, rt_ref, y0_ref, y1_ref, ms_ref, fw_ref, o_ref, *, final):
    w0 = rt_ref[0, :, 2:3]
    w1 = rt_ref[0, :, 3:4]
    x = x_ref[0] + ms_ref[0, 0, 5:6, :] * (w0 * y0_ref[...] + w1 * y1_ref[...])
    if final:
        var = jnp.mean(x * x, axis=-1, keepdims=True)
        x = x * lax.rsqrt(var + NORM_EPS) * fw_ref[...]
    o_ref[0] = x


def _combine(x_mid, route, y2, msel, fw, lc, tm, final):
    b, s, d = x_mid.shape
    off = lc // tm if final else 0
    nt = s // tm - off
    ntb = s // tm
    seg = lambda j: jnp.where((j + off) * tm >= lc, 1, 0)
    out_rows = s - lc if final else s
    return pl.pallas_call(
        functools.partial(_combine_kernel, final=final),
        grid=(b, nt),
        in_specs=[pl.BlockSpec((1, tm, d), lambda i, j: (i, j + off, 0)),
                  pl.BlockSpec((1, tm, 128), lambda i, j: (i, j + off, 0)),
                  pl.BlockSpec((tm, d), lambda i, j: (i * ntb + j + off, 0)),
                  pl.BlockSpec((tm, d), lambda i, j: (b * ntb + i * ntb + j + off, 0)),
                  pl.BlockSpec((1, 1, 6, d), lambda i, j: (i, seg(j), 0, 0)),
                  pl.BlockSpec((1, d), lambda i, j: (0, 0))],
        out_specs=pl.BlockSpec((1, tm, d), lambda i, j: (i, j, 0)),
        out_shape=jax.ShapeDtypeStruct((b, out_rows, d), F32),
        compiler_params=_cparams(("arbitrary", "arbitrary")),
        name="combine",
    )(x_mid, route, y2, y2, msel, fw)


def _rope_tables(n, lc):
    half = HEAD_DIM // 2
    inv = 1.0 / (ROPE_BASE ** (jnp.arange(0, half, 2, dtype=F32) / half))
    t = jnp.arange(n)
    ang_r = (t // GRID_W).astype(F32)[:, None] * inv[None, :]
    ang_c = (t % GRID_W).astype(F32)[:, None] * inv[None, :]
    cos_h = jnp.concatenate([jnp.cos(ang_r), jnp.cos(ang_r), jnp.cos(ang_c), jnp.cos(ang_c)], axis=-1)
    sin_h = jnp.concatenate([-jnp.sin(ang_r), jnp.sin(ang_r), -jnp.sin(ang_c), jnp.sin(ang_c)], axis=-1)
    cos_h = jnp.concatenate([jnp.ones((lc, HEAD_DIM), F32), cos_h], axis=0)
    sin_h = jnp.concatenate([jnp.zeros((lc, HEAD_DIM), F32), sin_h], axis=0)
    qs = HEAD_DIM ** -0.5
    cs = jnp.concatenate([jnp.tile(cos_h, (1, 4)) * qs, jnp.tile(cos_h, (1, 2))], axis=-1)
    sn = jnp.concatenate([jnp.tile(sin_h, (1, 4)) * qs, jnp.tile(sin_h, (1, 2))], axis=-1)
    return cs, sn


def _swap_perm(width):
    j = np.arange(width)
    return np.where((j % 32) < 16, j + 16, j - 16)


def kernel(x, c, ctx, c_ctx, mod_w, mod_b, norm1_w, norm2_w, w_in, w_out, swa_sink, na_rpb, hgrn_lb,
           hgrn_norm_w, s5_lam_re, s5_lam_im, s5_log_dt, s5_b_re, s5_b_im, s5_c_re, s5_c_im, s5_d,
           s5_glu_w, s5_glu_b, moe_group_w, moe_group_b, moe_expert_w, moe_expert_b, moe_w_gate,
           moe_w_up, moe_w_down, final_norm_w):
    bsz, n, d = x.shape
    lc = ctx.shape[1]
    s = lc + n
    depth = mod_w.shape[0]
    tm = min(TOKEN_TILE, lc)
    assert d == 1024 and lc % tm == 0 and n % tm == 0 and lc % HGRN_CHUNK == 0 and bsz % 8 == 0
    assert n % GRID_W == 0 and n // GRID_W >= NA_ROWS and n >= min(128, lc) + 2 * SWA_WINDOW

    x_all = jnp.concatenate([ctx, x], axis=1).astype(F32)

    rpad = (-(bsz + 1)) % 8
    c_all = jnp.concatenate([c, c_ctx[None], jnp.zeros((rpad, d), c.dtype)], axis=0).astype(F32)
    mods = _mods(c_all, mod_w.astype(F32), mod_b.astype(F32))
    mods = mods.reshape(depth, -1, 6, d)

    lbp = jax.nn.softmax(hgrn_lb.astype(F32), axis=1)
    lower = jnp.cumsum(lbp, axis=1) - lbp[:, :1]

    cs, sn = _rope_tables(n, lc)
    perm = _swap_perm(384)
    nch5 = s // S5_CHUNK

    out = None
    for l in range(depth):
        last = l == depth - 1
        need_ctx = not last
        msel = jnp.stack([jnp.broadcast_to(mods[l, bsz][None], (bsz, 6, d)), mods[l, :bsz]], axis=1)

        w = w_in[l]
        w_ext = jnp.concatenate([w[:, 0:384], w[:, 0:384][:, perm], w[:, 384:]], axis=1).astype(BF16)
        qa, qb, hg, u = _proj(x_all, msel, norm1_w[l].reshape(1, d).astype(F32), w_ext, cs, sn, lc, tm)

        a_out = _swa(qa, swa_sink[l].astype(F32), lc, need_ctx)
        b_out = _na(qb, _na_bias(na_rpb[l]), lc, need_ctx)
        o_f, o_b = _hgrn(hg, lower[:, l], lc)

        kmat, bmat, cmat, at, dvec = _s5_matrices(
            s5_lam_re[l], s5_lam_im[l], s5_log_dt[l], s5_b_re[l], s5_b_im[l], s5_c_re[l], s5_c_im[l], s5_d[l])
        ug = jnp.transpose(u.reshape(bsz, nch5, S5_CHUNK, S5_GROUPS, S5_GROUP_CH), (3, 1, 0, 2, 4))
        yg = _s5(ug.reshape(S5_GROUPS, nch5 * bsz, 256), kmat, bmat, cmat, at, dvec, bsz, lc // S5_CHUNK)
        y5 = jnp.transpose(yg.reshape(S5_GROUPS, nch5, bsz, S5_CHUNK, S5_GROUP_CH), (2, 1, 3, 0, 4))
        y5 = y5.reshape(bsz, s, 256)

        rw = jnp.concatenate([moe_group_w[l], moe_expert_w[l],
                              jnp.zeros((d, 128 - MOE_GROUPS - MOE_EXPERTS), F32)], axis=1).astype(F32)
        rb = jnp.concatenate([moe_group_b[l], moe_expert_b[l],
                              jnp.zeros((128 - MOE_GROUPS - MOE_EXPERTS,), F32)]).reshape(1, 128).astype(F32)
        x_mid, h2, route = _outproj(
            a_out, b_out, o_f, o_b, hg, y5, x_all, msel,
            jnp.tile(hgrn_norm_w[l].astype(F32), 4).reshape(1, 256),
            s5_glu_w[l].astype(BF16), s5_glu_b[l].reshape(1, 256).astype(F32),
            w_out[l].astype(BF16), norm2_w[l].reshape(1, d).astype(F32), rw, rb, lc, tm, need_ctx)

        t0 = 0 if need_ctx else lc
        eid = route[:, t0:, 0:2].astype(jnp.int32).reshape(-1, 2)
        rows = (jnp.arange(bsz, dtype=jnp.int32)[:, None] * s
                + jnp.arange(t0, s, dtype=jnp.int32)[None, :]).reshape(-1)
        te, row_src, row_dst, nact = _dispatch_plan(eid, rows, bsz * s, MOE_TILE)
        y2 = _moe(h2.reshape(bsz * s, d), moe_w_gate[l].astype(BF16), moe_w_up[l].astype(BF16),
                  moe_w_down[l].astype(BF16), te, row_src, row_dst, nact,
                  2 * bsz * s + 2 * MOE_TILE, MOE_TILE)
        res = _combine(x_mid, route, y2, msel, final_norm_w.reshape(1, d).astype(F32), lc, tm, last)
        if last:
            out = res
        else:
            x_all = res
    return out.astype(x.dtype)
```

```python
import functools
import math

import numpy as np
import jax
import jax.numpy as jnp
from jax import lax
from jax.experimental import pallas as pl
from jax.experimental.pallas import tpu as pltpu

F32 = jnp.float32
BF16 = jnp.bfloat16

GRID_W = 64
HEAD_DIM = 64
GROUP_WIDTH = 256
SWA_WINDOW = 128
ROPE_BASE = 10000.0
NA_ROWS = 8
NA_COLS = 16
HGRN_CHUNK = 64
HGRN_SUB = 16
S5_GROUP_CH = 16
S5_GROUPS = 16
S5_STATE = 64
S5_CHUNK = GROUP_WIDTH // S5_GROUP_CH
MOE_GROUPS = 4
MOE_PER_GROUP = 8
MOE_EXPERTS = 32
MOE_HIDDEN = 512
NORM_EPS = 1e-6
NEG_INF = -1e30
EXP_CLAMP = 80.0

V7X_VMEM_LIMIT_BYTES = 56 * 1024 * 1024
TOKEN_TILE = 256
MOE_TILE = 256


def _cparams(sem):
    return pltpu.CompilerParams(dimension_semantics=sem, vmem_limit_bytes=V7X_VMEM_LIMIT_BYTES)


def _dot(a, b):
    return jnp.dot(a, b, preferred_element_type=F32)


def _dot_nt(a, b):
    return lax.dot_general(a, b, (((1,), (1,)), ((), ())), preferred_element_type=F32)


def _dot_tn(a, b):
    return lax.dot_general(a, b, (((0,), (0,)), ((), ())), preferred_element_type=F32)


def _split(a):
    hi = a.astype(BF16)
    lo = (a - hi.astype(F32)).astype(BF16)
    return hi, lo


def _dot_f32(a, b):
    ah, al = _split(a)
    bh, bl = _split(b)
    return _dot(ah, bh) + _dot(al, bh) + _dot(ah, bl)


def _sigmoid(x):
    return 1.0 / (1.0 + jnp.exp(-x))


def _silu(x):
    return x * _sigmoid(x)


def _mod_kernel(c_ref, w_ref, b_ref, o_ref):
    c = c_ref[...]
    o_ref[0] = _dot_f32(_silu(c), w_ref[0]) + b_ref[0]


def _mods(c_all, mod_w, mod_b):
    depth, d, d6 = mod_w.shape
    r = c_all.shape[0]
    bn = 1024
    return pl.pallas_call(
        _mod_kernel,
        grid=(depth, d6 // bn),
        in_specs=[pl.BlockSpec((r, d), lambda l, j: (0, 0)),
                  pl.BlockSpec((1, d, bn), lambda l, j: (l, 0, j)),
                  pl.BlockSpec((1, 1, bn), lambda l, j: (l, 0, j))],
        out_specs=pl.BlockSpec((1, r, bn), lambda l, j: (l, 0, j)),
        out_shape=jax.ShapeDtypeStruct((depth, r, d6), F32),
        compiler_params=_cparams(("arbitrary", "arbitrary")),
        name="mods",
    )(c_all, mod_w, mod_b.reshape(depth, 1, d6))


def _proj_kernel(x_ref, ms_ref, nw_ref, w_ref, cs_ref, sn_ref, qa_ref, qb_ref, hg_ref, u_ref):
    x = x_ref[0]
    var = jnp.mean(x * x, axis=-1, keepdims=True)
    y = x * lax.rsqrt(var + NORM_EPS) * nw_ref[...]
    shift = ms_ref[0, 0, 0:1, :]
    scale = ms_ref[0, 0, 1:2, :]
    h = (y * (1.0 + scale) + shift).astype(BF16)
    r = _dot(h, w_ref[:, 0:768])
    roped = r[:, 0:384] * cs_ref[...] + r[:, 384:768] * sn_ref[...]
    qa_ref[0, :, 0:384] = roped.astype(BF16)
    t = _dot(h, w_ref[:, 768:1664])
    qa_ref[0, :, 384:512] = pltpu.roll(roped[:, 256:384], HEAD_DIM, 1).astype(BF16)
    qa_ref[0, :, 512:640] = t[:, 0:128].astype(BF16)
    qa_ref[0, :, 640:768] = pltpu.roll(t[:, 0:128], HEAD_DIM, 1).astype(BF16)
    qb_ref[0, :, 0:256] = (t[:, 128:384] * (HEAD_DIM ** -0.5)).astype(BF16)
    qb_ref[0, :, 256:768] = t[:, 384:896].astype(BF16)
    hg_ref[0] = _dot(h, w_ref[:, 1664:2944])
    u_ref[0] = _dot(h, w_ref[:, 2944:3200])


def _proj(x_all, msel, nw, w_ext, cs, sn, lc, tm):
    b, s, d = x_all.shape
    nt = s // tm
    seg = lambda j: jnp.where(j * tm >= lc, 1, 0)
    return pl.pallas_call(
        _proj_kernel,
        grid=(b, nt),
        in_specs=[pl.BlockSpec((1, tm, d), lambda i, j: (i, j, 0)),
                  pl.BlockSpec((1, 1, 6, d), lambda i, j: (i, seg(j), 0, 0)),
                  pl.BlockSpec((1, d), lambda i, j: (0, 0)),
                  pl.BlockSpec((d, 3200), lambda i, j: (0, 0)),
                  pl.BlockSpec((tm, 384), lambda i, j: (j, 0)),
                  pl.BlockSpec((tm, 384), lambda i, j: (j, 0))],
        out_specs=[pl.BlockSpec((1, tm, 768), lambda i, j: (i, j, 0)),
                   pl.BlockSpec((1, tm, 768), lambda i, j: (i, j, 0)),
                   pl.BlockSpec((1, tm, 1280), lambda i, j: (i, j, 0)),
                   pl.BlockSpec((1, tm, 256), lambda i, j: (i, j, 0))],
        out_shape=[jax.ShapeDtypeStruct((b, s, 768), BF16),
                   jax.ShapeDtypeStruct((b, s, 768), BF16),
                   jax.ShapeDtypeStruct((b, s, 1280), F32),
                   jax.ShapeDtypeStruct((b, s, 256), F32)],
        compiler_params=_cparams(("arbitrary", "arbitrary")),
        name="proj",
    )(x_all, msel, nw, w_ext, cs, sn)


def _attend(q, parts, sink):
    scores = []
    for k, _, mask in parts:
        s = _dot_nt(q, k)
        if mask is not None:
            s = s + mask
        scores.append(s)
    m = functools.reduce(jnp.maximum, [jnp.max(s, axis=-1, keepdims=True) for s in scores])
    if sink is not None:
        m = jnp.maximum(m, sink)
    den = jnp.zeros_like(m)
    out = None
    for s, (_, v, _) in zip(scores, parts):
        p = jnp.exp(s - m)
        den = den + jnp.sum(p, axis=-1, keepdims=True)
        o = _dot(p.astype(BF16), v)
        out = o if out is None else out + o
    if sink is not None:
        den = den + jnp.exp(sink - m)
    return out / den


def _swa_kernel(sink_ref, qa_ref, o_ref, *, lc, n, tq, nct):
    j = pl.program_id(1)
    win = tq + 2 * SWA_WINDOW
    lo = lax.broadcasted_iota(jnp.int32, (tq, 128), 1) < HEAD_DIM
    row_lo = lax.broadcasted_iota(jnp.int32, (2 * tq, 1), 0) < tq
    sink_a = jnp.where(row_lo, sink_ref[0], sink_ref[3])
    sink_b = jnp.where(row_lo, sink_ref[1], sink_ref[2])
    ctx = [qa_ref[0, 0:lc, 128 * t:128 * (t + 1)] for t in range(2, 6)]

    def heads(q, lat):
        zero = jnp.zeros_like(q[:, 0:128])
        q01, q23 = q[:, 0:128], q[:, 128:256]
        qa = jnp.concatenate([jnp.where(lo, q01, zero), jnp.where(lo, zero, q23)], axis=0)
        qb = jnp.concatenate([jnp.where(lo, zero, q01), jnp.where(lo, q23, zero)], axis=0)
        pa, pb = [], []
        if lat is not None:
            kw, kx, vw, vx, mask = lat
            mask2 = jnp.concatenate([mask, mask], axis=0)
            pa.append((kw, vw, mask2))
            pb.append((kx, vx, mask2))
        pa.append((ctx[0], ctx[2], None))
        pb.append((ctx[1], ctx[3], None))
        oa = _attend(qa, pa, sink_a)
        ob = _attend(qb, pb, sink_b)
        out01 = jnp.where(lo, oa[0:tq], ob[0:tq])
        out23 = jnp.where(lo, ob[tq:2 * tq], oa[tq:2 * tq])
        return jnp.concatenate([out01, out23], axis=-1).astype(BF16)

    if nct > 0:
        @pl.when(j < nct)
        def _():
            r0 = pl.multiple_of(j * tq, tq)
            o_ref[0] = heads(qa_ref[0, pl.ds(r0, tq), 0:256], None)

    @pl.when(j >= nct)
    def _():
        i = j - nct
        q0 = pl.multiple_of(lc + i * tq, 16)
        ks = jnp.clip(i * tq - SWA_WINDOW, 0, n - win)
        k0 = pl.multiple_of(lc + ks, 16)
        q = qa_ref[0, pl.ds(q0, tq), 0:256]
        lat = [qa_ref[0, pl.ds(k0, win), 128 * t:128 * (t + 1)] for t in range(2, 6)]
        qpos = i * tq + lax.broadcasted_iota(jnp.int32, (tq, win), 0)
        kpos = ks + lax.broadcasted_iota(jnp.int32, (tq, win), 1)
        mask = jnp.where(jnp.abs(qpos - kpos) <= SWA_WINDOW, 0.0, NEG_INF).astype(F32)
        o_ref[0] = heads(q, lat + [mask])


def _swa(qa, sink, lc, need_ctx):
    b, s, _ = qa.shape
    n = s - lc
    tq = min(128, lc)
    nct = lc // tq if need_ctx else 0
    return pl.pallas_call(
        functools.partial(_swa_kernel, lc=lc, n=n, tq=tq, nct=nct),
        grid=(b, nct + n // tq),
        in_specs=[pl.BlockSpec(memory_space=pltpu.SMEM),
                  pl.BlockSpec((1, s, 768), lambda i, j: (i, 0, 0))],
        out_specs=pl.BlockSpec((1, tq, 256), lambda i, j: (i, j, 0)),
        out_shape=jax.ShapeDtypeStruct((b, nct * tq + n, 256), BF16),
        compiler_params=_cparams(("arbitrary", "arbitrary")),
        name="swa",
    )(sink, qa)


def _na_kernel(qb_ref, bias_ref, o_ref, *, lc, rows, nct):
    j = pl.program_id(1)
    nk = NA_ROWS * GRID_W
    kc = qb_ref[0, 0:lc, 256:512]
    vc = qb_ref[0, 0:lc, 512:768]

    m = GRID_W
    rb = lax.broadcasted_iota(jnp.int32, (4 * m, 256), 0) // m
    lb = lax.broadcasted_iota(jnp.int32, (4 * m, 256), 1) // HEAD_DIM
    own = rb == lb

    def heads(q, lat):
        qbd = jnp.where(own, jnp.concatenate([q] * 4, axis=0), jnp.zeros((4 * m, 256), BF16))
        parts = []
        if lat is not None:
            kw, vw = lat
            parts.append((kw, vw, bias_ref[0]))
        parts.append((kc, vc, None))
        o4 = jnp.where(own, _attend(qbd, parts, None), 0.0)
        return (o4[0:m] + o4[m:2 * m] + o4[2 * m:3 * m] + o4[3 * m:4 * m]).astype(BF16)

    if nct > 0:
        @pl.when(j < nct)
        def _():
            r0 = pl.multiple_of(j * GRID_W, GRID_W)
            o_ref[0] = heads(qb_ref[0, pl.ds(r0, GRID_W), 0:256], None)

    @pl.when(j >= nct)
    def _():
        r = j - nct
        rs = jnp.clip(r - NA_ROWS // 2, 0, rows - NA_ROWS)
        q0 = pl.multiple_of(lc + r * GRID_W, GRID_W)
        k0 = pl.multiple_of(lc + rs * GRID_W, GRID_W)
        q = qb_ref[0, pl.ds(q0, GRID_W), 0:256]
        kw = qb_ref[0, pl.ds(k0, nk), 256:512]
        vw = qb_ref[0, pl.ds(k0, nk), 512:768]
        o_ref[0] = heads(q, (kw, vw))


def _na(qb, bias, lc, need_ctx):
    b, s, _ = qb.shape
    n = s - lc
    rows = n // GRID_W
    nct = lc // GRID_W if need_ctx else 0

    def cls(j):
        r = jnp.maximum(j - nct, 0)
        return r - jnp.clip(r - NA_ROWS // 2, 0, rows - NA_ROWS)

    return pl.pallas_call(
        functools.partial(_na_kernel, lc=lc, rows=rows, nct=nct),
        grid=(b, nct + rows),
        in_specs=[pl.BlockSpec((1, s, 768), lambda i, j: (i, 0, 0)),
                  pl.BlockSpec((1, 4 * GRID_W, NA_ROWS * GRID_W), lambda i, j: (cls(j), 0, 0))],
        out_specs=pl.BlockSpec((1, GRID_W, 256), lambda i, j: (i, j, 0)),
        out_shape=jax.ShapeDtypeStruct((b, nct * GRID_W + n, 256), BF16),
        compiler_params=_cparams(("arbitrary", "arbitrary")),
        name="na",
    )(qb, bias)


def _na_bias(rpb):
    c = np.arange(GRID_W)[:, None]
    kc = np.arange(GRID_W)[None, :]
    ci = np.clip(kc - c + NA_COLS - 1, 0, 2 * NA_COLS - 2)
    ws = np.clip(c - NA_COLS // 2, 0, GRID_W - NA_COLS)
    valid = (kc >= ws) & (kc < ws + NA_COLS)
    sel = (ci[None] == np.arange(2 * NA_COLS - 1)[:, None, None]).astype(np.float32)
    t = jnp.einsum('hrd,dck->hrck', rpb.astype(F32), sel, precision=lax.Precision.HIGHEST)
    t = jnp.where(valid[None, None], t, NEG_INF)
    per_cls = [t[:, NA_ROWS - 1 - d:2 * NA_ROWS - 1 - d] for d in range(NA_ROWS)]
    t = jnp.transpose(jnp.stack(per_cls, axis=0), (0, 1, 3, 2, 4))
    return t.reshape(NA_ROWS, 4 * GRID_W, NA_ROWS * GRID_W)


def _hgrn_dir(blk, lb, st_ref, rev):
    c = HGRN_CHUNK
    sub = HGRN_SUB
    nsub = c // sub
    q = _silu(blk[:, 0:256])
    fraw = blk[:, 512:768] if rev else blk[:, 256:512]
    v = blk[:, 768:1024]
    f = lb + (1.0 - lb) * _sigmoid(fraw)
    kk = 1.0 - f
    logf = jnp.log(f)
    ti = lax.broadcasted_iota(jnp.int32, (c, c), 0)
    si = lax.broadcasted_iota(jnp.int32, (c, c), 1)
    tri = jnp.where((si >= ti) if rev else (si <= ti), 1.0, 0.0).astype(BF16)
    bcum = _dot_exact_rhs_left(tri, logf)
    btot = bcum[0:1] if rev else bcum[c - 1:c]
    vb = v.astype(BF16)
    st = st_ref[...]
    o_inter = _dot_nt((q * jnp.exp(bcum)).astype(BF16), st.astype(BF16))

    rh = lax.broadcasted_iota(jnp.int32, (c, 256), 0) // sub
    lh = lax.broadcasted_iota(jnp.int32, (c, 256), 1) // HEAD_DIM
    bd = jnp.where(rh == lh, 1.0, 0.0).astype(F32)
    outs = []
    for i in range(nsub):
        r0, r1 = i * sub, (i + 1) * sub
        if rev:
            rho = bcum[r1:r1 + 1] if i < nsub - 1 else jnp.zeros((1, 256), F32)
            k0, k1 = r0, c
        else:
            rho = bcum[r0 - 1:r0] if i > 0 else jnp.zeros((1, 256), F32)
            k0, k1 = 0, r1
        qh = q[r0:r1] * jnp.exp(bcum[r0:r1] - rho)
        kh = kk[k0:k1] * jnp.exp(jnp.minimum(rho - bcum[k0:k1], EXP_CLAMP))
        qbd = (jnp.concatenate([qh] * 4, axis=0) * bd).astype(BF16)
        a = _dot_nt(qbd, kh.astype(BF16))
        nk = k1 - k0
        tq = r0 + lax.broadcasted_iota(jnp.int32, (c, nk), 0) % sub
        sk = k0 + lax.broadcasted_iota(jnp.int32, (c, nk), 1)
        a = jnp.where((sk >= tq) if rev else (sk <= tq), a, 0.0)
        o4 = _dot(a.astype(BF16), vb[k0:k1]) * bd
        outs.append(o4[0:sub] + o4[sub:2 * sub] + o4[2 * sub:3 * sub] + o4[3 * sub:4 * sub])
    o = o_inter + jnp.concatenate(outs, axis=0)

    kend = (kk * jnp.exp(btot - bcum)).astype(BF16)
    r2 = lax.broadcasted_iota(jnp.int32, (256, 256), 0) // HEAD_DIM
    l2 = lax.broadcasted_iota(jnp.int32, (256, 256), 1) // HEAD_DIM
    upd = jnp.where(r2 == l2, _dot_tn(vb, kend), 0.0)
    st_ref[...] = st * jnp.exp(btot) + upd
    return o


def _dot_exact_rhs_left(tri_bf16, a):
    a0 = a.astype(BF16)
    r1 = a - a0.astype(F32)
    a1 = r1.astype(BF16)
    a2 = (r1 - a1.astype(F32)).astype(BF16)
    return _dot(tri_bf16, a0) + _dot(tri_bf16, a1) + _dot(tri_bf16, a2)


def _hgrn_kernel(hf_ref, hb_ref, lb_ref, of_ref, ob_ref, stf_ref, stb_ref):
    @pl.when(pl.program_id(1) == 0)
    def _():
        stf_ref[...] = jnp.zeros_like(stf_ref)
        stb_ref[...] = jnp.zeros_like(stb_ref)

    of_ref[0] = _hgrn_dir(hf_ref[0], lb_ref[0:1, :], stf_ref, False)
    ob_ref[0] = _hgrn_dir(hb_ref[0], lb_ref[1:2, :], stb_ref, True)


def _hgrn(hg, lb2, lc):
    b, s, _ = hg.shape
    c = HGRN_CHUNK
    nc = s // c
    ncc = lc // c

    def bwd(j):
        return jnp.where(j < ncc, ncc - 1 - j, ncc + nc - 1 - j)

    return pl.pallas_call(
        _hgrn_kernel,
        grid=(b, nc),
        in_specs=[pl.BlockSpec((1, c, 1280), lambda i, j: (i, j, 0)),
                  pl.BlockSpec((1, c, 1280), lambda i, j: (i, bwd(j), 0)),
                  pl.BlockSpec((2, 256), lambda i, j: (0, 0))],
        out_specs=[pl.BlockSpec((1, c, 256), lambda i, j: (i, j, 0)),
                   pl.BlockSpec((1, c, 256), lambda i, j: (i, bwd(j), 0))],
        out_shape=[jax.ShapeDtypeStruct((b, s, 256), F32),
                   jax.ShapeDtypeStruct((b, s, 256), F32)],
        scratch_shapes=[pltpu.VMEM((256, 256), F32), pltpu.VMEM((256, 256), F32)],
        compiler_params=_cparams(("arbitrary", "arbitrary")),
        name="hgrn",
    )(hg, hg, lb2)


def _s5_kernel(u_ref, k_ref, bm_ref, cm_ref, at_ref, d_ref, y_ref,
               hin_ref, hst_ref, *, bsz, ncc, nc):
    u = u_ref[0]
    ub = u.astype(BF16)
    for m in range(4):
        hin_ref[m] = _dot(ub, bm_ref[0, m])

    def run(direction, order_fn, count, h0):
        a_re = at_ref[0, 2 * direction:2 * direction + 1, :]
        a_im = at_ref[0, 2 * direction + 1:2 * direction + 2, :]

        def body(t, carry):
            h_re, h_im = carry
            r0 = pl.multiple_of(order_fn(t) * bsz, 8)
            hst_ref[2 * direction, pl.ds(r0, bsz), :] = h_re
            hst_ref[2 * direction + 1, pl.ds(r0, bsz), :] = h_im
            n_re = a_re * h_re - a_im * h_im + hin_ref[2 * direction, pl.ds(r0, bsz), :]
            n_im = a_re * h_im + a_im * h_re + hin_ref[2 * direction + 1, pl.ds(r0, bsz), :]
            return n_re, n_im

        return lax.fori_loop(0, count, body, h0)

    z = jnp.zeros((bsz, S5_STATE), F32)
    run(0, lambda t: t, nc, (z, z))
    hb = run(1, lambda t: ncc - 1 - t, ncc, (z, z))
    run(1, lambda t: nc - 1 - t, nc - ncc, hb)

    y = _dot(ub, k_ref[0]) + d_ref[0] * u
    for m in range(4):
        y = y + _dot(hst_ref[m].astype(BF16), cm_ref[0, m])
    y_ref[0] = y


def _s5(ug, kmat, bmat, cmat, at, dvec, bsz, ncc):
    g, rows, _ = ug.shape
    nc = rows // bsz
    return pl.pallas_call(
        functools.partial(_s5_kernel, bsz=bsz, ncc=ncc, nc=nc),
        grid=(g,),
        in_specs=[pl.BlockSpec((1, rows, 256), lambda i: (i, 0, 0)),
                  pl.BlockSpec((1, 256, 256), lambda i: (i, 0, 0)),
                  pl.BlockSpec((1, 4, 256, S5_STATE), lambda i: (i, 0, 0, 0)),
                  pl.BlockSpec((1, 4, S5_STATE, 256), lambda i: (i, 0, 0, 0)),
                  pl.BlockSpec((1, 4, S5_STATE), lambda i: (i, 0, 0)),
                  pl.BlockSpec((1, 1, 256), lambda i: (i, 0, 0))],
        out_specs=pl.BlockSpec((1, rows, 256), lambda i: (i, 0, 0)),
        out_shape=jax.ShapeDtypeStruct((g, rows, 256), F32),
        scratch_shapes=[pltpu.VMEM((4, rows, S5_STATE), F32), pltpu.VMEM((4, rows, S5_STATE), F32)],
        compiler_params=_cparams(("arbitrary",)),
        name="s5",
    )(ug, kmat, bmat, cmat, at, dvec)


def _s5_matrices(lam_re, lam_im, log_dt, b_re, b_im, c_re, c_im, d):
    t = S5_CHUNK
    hp = lax.Precision.HIGHEST
    lam_re, lam_im, log_dt = lam_re.astype(F32), lam_im.astype(F32), log_dt.astype(F32)
    b_re, b_im, c_re, c_im = b_re.astype(F32), b_im.astype(F32), c_re.astype(F32), c_im.astype(F32)
    dt = jnp.exp(log_dt)[..., None]
    jj = jnp.arange(t + 1, dtype=F32)[:, None, None, None]
    mag = jnp.exp(lam_re * dt * jj)
    ang = lam_im * dt * jj
    aj_re, aj_im = mag * jnp.cos(ang), mag * jnp.sin(ang)
    a_re, a_im = aj_re[1], aj_im[1]
    den = lam_re * lam_re + lam_im * lam_im
    k_re = ((a_re - 1.0) * lam_re + a_im * lam_im) / den
    k_im = (a_im * lam_re - (a_re - 1.0) * lam_im) / den
    bb_re = k_re[..., None] * b_re - k_im[..., None] * b_im
    bb_im = k_re[..., None] * b_im + k_im[..., None] * b_re
    ab_re = aj_re[..., None] * bb_re - aj_im[..., None] * bb_im
    ab_im = aj_re[..., None] * bb_im + aj_im[..., None] * bb_re
    kj = (jnp.einsum('dghp,jdgpi->jdghi', c_re, ab_re, precision=hp)
          - jnp.einsum('dghp,jdgpi->jdghi', c_im, ab_im, precision=hp))
    s_i = np.arange(t)[:, None]
    t_i = np.arange(t)[None, :]
    kf = jnp.where((t_i >= s_i)[:, :, None, None, None], kj[np.clip(t_i - s_i, 0, t), 0], 0.0)
    kb = jnp.where((s_i >= t_i)[:, :, None, None, None], kj[np.clip(s_i - t_i, 0, t), 1], 0.0)
    kmat = jnp.transpose(kf + kb, (2, 0, 4, 1, 3)).reshape(S5_GROUPS, 256, 256)

    def state_in(ab, idx, direction):
        return jnp.transpose(ab[idx, direction], (1, 0, 3, 2)).reshape(S5_GROUPS, 256, S5_STATE)

    fw_idx = np.arange(t - 1, -1, -1)
    bw_idx = np.arange(t)
    bmat = jnp.stack([state_in(ab_re, fw_idx, 0), state_in(ab_im, fw_idx, 0),
                      state_in(ab_re, bw_idx, 1), state_in(ab_im, bw_idx, 1)], axis=1)

    def state_out(idx, direction):
        ar = aj_re[idx, direction][:, :, None, :]
        ai = aj_im[idx, direction][:, :, None, :]
        cr, ci = c_re[direction][None], c_im[direction][None]
        re = cr * ar - ci * ai
        im = -(cr * ai + ci * ar)
        f = lambda m: jnp.transpose(m, (1, 3, 0, 2)).reshape(S5_GROUPS, S5_STATE, 256)
        return f(re), f(im)

    cf_re, cf_im = state_out(np.arange(1, t + 1), 0)
    cb_re, cb_im = state_out(np.arange(t, 0, -1), 1)
    cmat = jnp.stack([cf_re, cf_im, cb_re, cb_im], axis=1)
    at = jnp.stack([aj_re[t, 0], aj_im[t, 0], aj_re[t, 1], aj_im[t, 1]], axis=1)
    dvec = jnp.tile(d.astype(F32).reshape(S5_GROUPS, 1, S5_GROUP_CH), (1, t, 1)).reshape(S5_GROUPS, 1, 256)
    return kmat.astype(BF16), bmat.astype(BF16), cmat.astype(BF16), at, dvec


def _gelu_tanh(x):
    return 0.5 * x * (1.0 + jnp.tanh(math.sqrt(2.0 / math.pi) * (x + 0.044715 * (x * x * x))))


def _outproj_kernel(a_ref, b_ref, of_ref, ob_ref, g_ref, y5_ref, x_ref, ms_ref, hnw_ref,
                    gw_ref, gb_ref, wo_ref, n2_ref, rw_ref, rb_ref,
                    xo_ref, h2_ref, rt_ref, cnt_ref, run_ref):
    @pl.when((pl.program_id(0) == 0) & (pl.program_id(1) == 0))
    def _():
        run_ref[...] = jnp.zeros_like(run_ref)

    o = of_ref[0] + ob_ref[0]
    r2 = lax.broadcasted_iota(jnp.int32, (256, 256), 0) // HEAD_DIM
    l2 = lax.broadcasted_iota(jnp.int32, (256, 256), 1) // HEAD_DIM
    hm = jnp.where(r2 == l2, 1.0, 0.0).astype(BF16)
    sq = o * o
    sh, sl = _split(sq)
    ms = (_dot(sh, hm) + _dot(sl, hm)) * (1.0 / HEAD_DIM)
    cm = o * lax.rsqrt(ms + NORM_EPS) * hnw_ref[...] * _silu(g_ref[0])
    z = _gelu_tanh(y5_ref[0])
    dm = z * _sigmoid(_dot(z.astype(BF16), gw_ref[...]) + gb_ref[...])
    mix = (_dot(a_ref[0], wo_ref[0:256, :]) + _dot(b_ref[0], wo_ref[256:512, :])
           + _dot(cm.astype(BF16), wo_ref[512:768, :]) + _dot(dm.astype(BF16), wo_ref[768:1024, :]))
    x = x_ref[0] + ms_ref[0, 0, 2:3, :] * mix
    xo_ref[0] = x
    var = jnp.mean(x * x, axis=-1, keepdims=True)
    h2 = x * lax.rsqrt(var + NORM_EPS) * n2_ref[...]
    h2 = h2 * (1.0 + ms_ref[0, 0, 4:5, :]) + ms_ref[0, 0, 3:4, :]
    h2_ref[0] = h2
    lg = _dot_f32(h2, rw_ref[...]) + rb_ref[...]
    tm = lg.shape[0]
    lane = lax.broadcasted_iota(jnp.int32, (tm, 128), 1)
    big = jnp.int32(1 << 20)
    ninf = jnp.float32(-jnp.inf)
    gmask = lane < MOE_GROUPS
    gl = jnp.where(gmask, lg, ninf)
    gmax = jnp.max(gl, axis=-1, keepdims=True)
    gidx = jnp.min(jnp.where(gl == gmax, lane, big), axis=-1, keepdims=True)
    gsum = jnp.sum(jnp.where(gmask, jnp.exp(gl - gmax), 0.0), axis=-1, keepdims=True)
    gwt = 1.0 / gsum
    e = lane - MOE_GROUPS
    emask = (e >= 0) & (e < MOE_EXPERTS) & (lax.shift_right_arithmetic(e, 3) == gidx)
    el = jnp.where(emask, lg, ninf)
    v1 = jnp.max(el, axis=-1, keepdims=True)
    i1 = jnp.min(jnp.where(el == v1, lane, big), axis=-1, keepdims=True)
    el2 = jnp.where(lane == i1, ninf, el)
    v2 = jnp.max(el2, axis=-1, keepdims=True)
    i2 = jnp.min(jnp.where(el2 == v2, lane, big), axis=-1, keepdims=True)
    t2 = jnp.exp(v2 - v1)
    w1 = gwt / (1.0 + t2)
    w2 = w1 * t2
    pick1 = lane == i1
    pick2 = lane == i2
    oh = jnp.where(pick1 | pick2, 1.0, 0.0).astype(F32)
    ti = lax.broadcasted_iota(jnp.int32, (tm, tm), 0)
    si = lax.broadcasted_iota(jnp.int32, (tm, tm), 1)
    before = _dot(jnp.where(si < ti, 1.0, 0.0).astype(BF16), oh.astype(BF16)) + run_ref[0:1, :]
    r1 = jnp.sum(jnp.where(pick1, before, 0.0), axis=-1, keepdims=True)
    r2 = jnp.sum(jnp.where(pick2, before, 0.0), axis=-1, keepdims=True)
    run = run_ref[...] + jnp.sum(oh, axis=0, keepdims=True)
    run_ref[...] = run
    cnt_ref[...] = run
    vals = ((i1 - MOE_GROUPS).astype(F32), (i2 - MOE_GROUPS).astype(F32), w1, w2, r1, r2)
    rt = jnp.zeros((tm, 128), F32)
    for k, val in enumerate(vals):
        rt = jnp.where(lane == k, val, rt)
    rt_ref[0] = rt


def _outproj(a_out, b_out, o_f, o_b, hg, y5, x_all, msel, hnw, glu_w, glu_b, w_out, n2w, rw, rb,
             lc, tm, need_ctx):
    b, s, d = x_all.shape
    off = 0 if need_ctx else lc // tm
    nt = s // tm - off
    rows = nt * tm
    seg = lambda j: jnp.where((j + off) * tm >= lc, 1, 0)
    tok = lambda w: pl.BlockSpec((1, tm, w), lambda i, j: (i, j + off, 0))
    loc = lambda w: pl.BlockSpec((1, tm, w), lambda i, j: (i, j, 0))
    full = lambda r, c: pl.BlockSpec((r, c), lambda i, j: (0, 0))
    return pl.pallas_call(
        _outproj_kernel,
        grid=(b, nt),
        in_specs=[loc(256), loc(256), tok(256), tok(256),
                  pl.BlockSpec((1, tm, 256), lambda i, j: (i, j + off, 4)),
                  tok(256), tok(d),
                  pl.BlockSpec((1, 1, 6, d), lambda i, j: (i, seg(j), 0, 0)),
                  full(1, 256), full(256, 256), full(1, 256), full(d, d), full(1, d),
                  full(d, 128), full(1, 128)],
        out_specs=[loc(d), loc(d), loc(128), full(8, 128)],
        out_shape=[jax.ShapeDtypeStruct((b, rows, d), F32),
                   jax.ShapeDtypeStruct((b, rows, d), F32),
                   jax.ShapeDtypeStruct((b, rows, 128), F32),
                   jax.ShapeDtypeStruct((8, 128), F32)],
        scratch_shapes=[pltpu.VMEM((8, 128), F32)],
        compiler_params=_cparams(("arbitrary", "arbitrary")),
        name="outproj",
    )(a_out, b_out, o_f, o_b, hg, y5, x_all, msel, hnw, glu_w, glu_b, w_out, n2w, rw, rb)


DMA_UNROLL = 8


def _dispatch_kernel(slot_ref, h_ref, xs_in, xs_out, sem, *, tm):
    del xs_in

    def row_copy(r, k, dst):
        return pltpu.make_async_copy(h_ref.at[pl.ds(r, 1)], xs_out.at[pl.ds(dst, 1)], sem.at[k])

    def issue(r, c):
        row_copy(r, 0, slot_ref[0, 0, r]).start()
        row_copy(r, 1, slot_ref[0, 1, r]).start()
        return c

    lax.fori_loop(0, tm, issue, 0, unroll=DMA_UNROLL)
    for k in range(2):
        pltpu.make_async_copy(h_ref, xs_out.at[pl.ds(0, tm)], sem.at[k]).wait()


def _dispatch(h2_flat, slots3, n_slots, tm):
    rows, d = h2_flat.shape
    xs0 = jnp.zeros((n_slots, d), F32)
    return pl.pallas_call(
        functools.partial(_dispatch_kernel, tm=tm),
        grid=(rows // tm,),
        in_specs=[pl.BlockSpec((1, 2, tm), lambda i: (i, 0, 0), memory_space=pltpu.SMEM),
                  pl.BlockSpec((tm, d), lambda i: (i, 0)),
                  pl.BlockSpec(memory_space=pl.ANY)],
        out_specs=pl.BlockSpec(memory_space=pl.ANY),
        out_shape=jax.ShapeDtypeStruct((n_slots, d), F32),
        scratch_shapes=[pltpu.SemaphoreType.DMA((2,))],
        input_output_aliases={2: 0},
        compiler_params=_cparams(("arbitrary",)),
        name="dispatch",
    )(slots3, h2_flat, xs0)


def _moe_kernel(te_ref, nact_ref, x_ref, wg_ref, wu_ref, wd_ref, y_ref):
    del te_ref
    i = pl.program_id(0)

    @pl.when(i < nact_ref[0])
    def _():
        x = x_ref[...].astype(BF16)
        hid = _silu(_dot(x, wg_ref[0])) * _dot(x, wu_ref[0])
        y_ref[...] = _dot(hid.astype(BF16), wd_ref[0])

    @pl.when(i >= nact_ref[0])
    def _():
        y_ref[...] = jnp.zeros_like(y_ref)


def _moe(xs, wg, wu, wd, tile_expert, nact, tm):
    n_slots, d = xs.shape
    hid = wg.shape[2]
    nt = n_slots // tm
    return pl.pallas_call(
        _moe_kernel,
        grid_spec=pltpu.PrefetchScalarGridSpec(
            num_scalar_prefetch=2,
            grid=(nt,),
            in_specs=[pl.BlockSpec((tm, d), lambda i, te, na: (i, 0)),
                      pl.BlockSpec((1, d, hid), lambda i, te, na: (te[i], 0, 0)),
                      pl.BlockSpec((1, d, hid), lambda i, te, na: (te[i], 0, 0)),
                      pl.BlockSpec((1, hid, d), lambda i, te, na: (te[i], 0, 0))],
            out_specs=pl.BlockSpec((tm, d), lambda i, te, na: (i, 0))),
        out_shape=jax.ShapeDtypeStruct((n_slots, d), F32),
        compiler_params=_cparams(("arbitrary",)),
        name="moe",
    )(tile_expert, nact, xs, wg, wu, wd)


def _dispatch_plan(route, counts, tok_tile, moe_tile):
    t = route.shape[0]
    eid = route[:, 0:2].astype(jnp.int32)
    rank = route[:, 4:6].astype(jnp.int32)
    padded = ((counts + moe_tile - 1) // moe_tile) * moe_tile
    pad_end = jnp.cumsum(padded)
    pad_start = pad_end - padded
    slots = jnp.take(pad_start, eid, axis=0) + rank
    slots3 = jnp.transpose(slots.reshape(t // tok_tile, tok_tile, 2), (0, 2, 1))
    nt = (2 * t) // moe_tile + MOE_EXPERTS
    nact = (pad_end[-1] // moe_tile).astype(jnp.int32)
    tile_start = jnp.arange(nt, dtype=jnp.int32) * moe_tile
    te = jnp.sum((pad_end[None, :] <= tile_start[:, None]).astype(jnp.int32), axis=1)
    last_used = jnp.max(jnp.where(counts > 0, jnp.arange(MOE_EXPERTS, dtype=jnp.int32), 0))
    te = jnp.minimum(te, last_used)
    return slots3, te, nact.reshape(1), nt * moe_tile


def _combine_kernel(slot_ref, nslot_ref, x_ref, rt_ref, ms_ref, fw_ref, ys_hbm, o_ref, ybuf, sem,
                    *, final, tm):
    i = pl.program_id(0)
    n = pl.num_programs(0)
    cur = lax.rem(i, 2)

    def row_copy(src, buf, k, r):
        return pltpu.make_async_copy(ys_hbm.at[pl.ds(src, 1)], ybuf.at[buf, k, pl.ds(r, 1)], sem.at[buf])

    def issue(table, buf):
        def body(r, c):
            row_copy(table[0, 0, r], buf, 0, r).start()
            row_copy(table[0, 1, r], buf, 1, r).start()
            return c
        lax.fori_loop(0, tm, body, 0, unroll=DMA_UNROLL)

    @pl.when(i == 0)
    def _():
        issue(slot_ref, 0)

    @pl.when(i + 1 < n)
    def _():
        issue(nslot_ref, 1 - cur)

    for k in range(2):
        pltpu.make_async_copy(ys_hbm.at[pl.ds(0, tm)], ybuf.at[cur, k], sem.at[cur]).wait()

    w0 = rt_ref[:, 2:3]
    w1 = rt_ref[:, 3:4]
    x = x_ref[...] + ms_ref[0, 0, 5:6, :] * (w0 * ybuf[cur, 0] + w1 * ybuf[cur, 1])
    if final:
        var = jnp.mean(x * x, axis=-1, keepdims=True)
        x = x * lax.rsqrt(var + NORM_EPS) * fw_ref[...]
    o_ref[...] = x


def _combine(x_mid, route, ys, slots3, msel, fw, rows_per_batch, ctx_rows, tm, final):
    rows, d = x_mid.shape
    nt = rows // tm
    ntb = rows_per_batch // tm
    seg = lambda i: jnp.where((i % ntb) * tm >= ctx_rows, 1, 0)
    smem_tile = lambda fn: pl.BlockSpec((1, 2, tm), fn, memory_space=pltpu.SMEM)
    return pl.pallas_call(
        functools.partial(_combine_kernel, final=final, tm=tm),
        grid=(nt,),
        in_specs=[smem_tile(lambda i: (i, 0, 0)),
                  smem_tile(lambda i: (jnp.minimum(i + 1, nt - 1), 0, 0)),
                  pl.BlockSpec((tm, d), lambda i: (i, 0)),
                  pl.BlockSpec((tm, 128), lambda i: (i, 0)),
                  pl.BlockSpec((1, 1, 6, d), lambda i: (i // ntb, seg(i), 0, 0)),
                  pl.BlockSpec((1, d), lambda i: (0, 0)),
                  pl.BlockSpec(memory_space=pl.ANY)],
        out_specs=pl.BlockSpec((tm, d), lambda i: (i, 0)),
        out_shape=jax.ShapeDtypeStruct((rows, d), F32),
        scratch_shapes=[pltpu.VMEM((2, 2, tm, d), F32), pltpu.SemaphoreType.DMA((2,))],
        compiler_params=_cparams(("arbitrary",)),
        name="combine",
    )(slots3, slots3, x_mid, route, msel, fw, ys)


def _rope_tables(n, lc):
    half = HEAD_DIM // 2
    inv = 1.0 / (ROPE_BASE ** (jnp.arange(0, half, 2, dtype=F32) / half))
    t = jnp.arange(n)
    ang_r = (t // GRID_W).astype(F32)[:, None] * inv[None, :]
    ang_c = (t % GRID_W).astype(F32)[:, None] * inv[None, :]
    cos_h = jnp.concatenate([jnp.cos(ang_r), jnp.cos(ang_r), jnp.cos(ang_c), jnp.cos(ang_c)], axis=-1)
    sin_h = jnp.concatenate([-jnp.sin(ang_r), jnp.sin(ang_r), -jnp.sin(ang_c), jnp.sin(ang_c)], axis=-1)
    cos_h = jnp.concatenate([jnp.ones((lc, HEAD_DIM), F32), cos_h], axis=0)
    sin_h = jnp.concatenate([jnp.zeros((lc, HEAD_DIM), F32), sin_h], axis=0)
    qs = HEAD_DIM ** -0.5
    cs = jnp.concatenate([jnp.tile(cos_h, (1, 4)) * qs, jnp.tile(cos_h, (1, 2))], axis=-1)
    sn = jnp.concatenate([jnp.tile(sin_h, (1, 4)) * qs, jnp.tile(sin_h, (1, 2))], axis=-1)
    return cs, sn


def _swap_perm(width):
    j = np.arange(width)
    return np.where((j % 32) < 16, j + 16, j - 16)


def kernel(x, c, ctx, c_ctx, mod_w, mod_b, norm1_w, norm2_w, w_in, w_out, swa_sink, na_rpb, hgrn_lb,
           hgrn_norm_w, s5_lam_re, s5_lam_im, s5_log_dt, s5_b_re, s5_b_im, s5_c_re, s5_c_im, s5_d,
           s5_glu_w, s5_glu_b, moe_group_w, moe_group_b, moe_expert_w, moe_expert_b, moe_w_gate,
           moe_w_up, moe_w_down, final_norm_w):
    bsz, n, d = x.shape
    lc = ctx.shape[1]
    s = lc + n
    depth = mod_w.shape[0]
    tm = min(TOKEN_TILE, lc)
    assert d == 1024 and lc % tm == 0 and n % tm == 0 and lc % HGRN_CHUNK == 0 and bsz % 8 == 0
    assert n % GRID_W == 0 and n // GRID_W >= NA_ROWS and n >= min(128, lc) + 2 * SWA_WINDOW

    x_all = jnp.concatenate([ctx, x], axis=1).astype(F32)

    rpad = (-(bsz + 1)) % 8
    c_all = jnp.concatenate([c, c_ctx[None], jnp.zeros((rpad, d), c.dtype)], axis=0).astype(F32)
    mods = _mods(c_all, mod_w.astype(F32), mod_b.astype(F32))
    mods = mods.reshape(depth, -1, 6, d)

    lbp = jax.nn.softmax(hgrn_lb.astype(F32), axis=1)
    lower = jnp.cumsum(lbp, axis=1) - lbp[:, :1]

    cs, sn = _rope_tables(n, lc)
    perm = _swap_perm(384)
    nch5 = s // S5_CHUNK

    out = None
    for l in range(depth):
        last = l == depth - 1
        need_ctx = not last
        msel = jnp.stack([jnp.broadcast_to(mods[l, bsz][None], (bsz, 6, d)), mods[l, :bsz]], axis=1)

        w = w_in[l]
        w_ext = jnp.concatenate([w[:, 0:384], w[:, 0:384][:, perm], w[:, 384:]], axis=1).astype(BF16)
        qa, qb, hg, u = _proj(x_all, msel, norm1_w[l].reshape(1, d).astype(F32), w_ext, cs, sn, lc, tm)

        a_out = _swa(qa, swa_sink[l].astype(F32), lc, need_ctx)
        b_out = _na(qb, _na_bias(na_rpb[l]), lc, need_ctx)
        o_f, o_b = _hgrn(hg, lower[:, l], lc)

        kmat, bmat, cmat, at, dvec = _s5_matrices(
            s5_lam_re[l], s5_lam_im[l], s5_log_dt[l], s5_b_re[l], s5_b_im[l], s5_c_re[l], s5_c_im[l], s5_d[l])
        ug = jnp.transpose(u.reshape(bsz, nch5, S5_CHUNK, S5_GROUPS, S5_GROUP_CH), (3, 1, 0, 2, 4))
        yg = _s5(ug.reshape(S5_GROUPS, nch5 * bsz, 256), kmat, bmat, cmat, at, dvec, bsz, lc // S5_CHUNK)
        y5 = jnp.transpose(yg.reshape(S5_GROUPS, nch5, bsz, S5_CHUNK, S5_GROUP_CH), (2, 1, 3, 0, 4))
        y5 = y5.reshape(bsz, s, 256)

        rw = jnp.concatenate([moe_group_w[l], moe_expert_w[l],
                              jnp.zeros((d, 128 - MOE_GROUPS - MOE_EXPERTS), F32)], axis=1).astype(F32)
        rb = jnp.concatenate([moe_group_b[l], moe_expert_b[l],
                              jnp.zeros((128 - MOE_GROUPS - MOE_EXPERTS,), F32)]).reshape(1, 128).astype(F32)
        x_mid, h2, route, cnt = _outproj(
            a_out, b_out, o_f, o_b, hg, y5, x_all, msel,
            jnp.tile(hgrn_norm_w[l].astype(F32), 4).reshape(1, 256),
            s5_glu_w[l].astype(BF16), s5_glu_b[l].reshape(1, 256).astype(F32),
            w_out[l].astype(BF16), norm2_w[l].reshape(1, d).astype(F32), rw, rb, lc, tm, need_ctx)

        rpb_rows = x_mid.shape[1]
        route2 = route.reshape(bsz * rpb_rows, 128)
        counts = cnt[0, MOE_GROUPS:MOE_GROUPS + MOE_EXPERTS].astype(jnp.int32)
        slots3, te, nact, n_slots = _dispatch_plan(route2, counts, tm, MOE_TILE)
        xs = _dispatch(h2.reshape(bsz * rpb_rows, d), slots3, n_slots, tm)
        ys = _moe(xs, moe_w_gate[l].astype(BF16), moe_w_up[l].astype(BF16), moe_w_down[l].astype(BF16),
                  te, nact, MOE_TILE)
        res = _combine(x_mid.reshape(bsz * rpb_rows, d), route2, ys, slots3, msel,
                       final_norm_w.reshape(1, d).astype(F32), rpb_rows, lc if need_ctx else 0, tm, last)
        res = res.reshape(bsz, rpb_rows, d)
        if last:
            out = res
        else:
            x_all = res
    return out.astype(x.dtype)
```

```python
import functools
import math

import numpy as np
import jax
import jax.numpy as jnp
from jax import lax
from jax.experimental import pallas as pl
from jax.experimental.pallas import tpu as pltpu

F32 = jnp.float32
BF16 = jnp.bfloat16

GRID_W = 64
HEAD_DIM = 64
GROUP_WIDTH = 256
SWA_WINDOW = 128
ROPE_BASE = 10000.0
NA_ROWS = 8
NA_COLS = 16
HGRN_CHUNK = 64
HGRN_SUB = 16
S5_GROUP_CH = 16
S5_GROUPS = 16
S5_STATE = 64
S5_CHUNK = GROUP_WIDTH // S5_GROUP_CH
MOE_GROUPS = 4
MOE_PER_GROUP = 8
MOE_EXPERTS = 32
MOE_HIDDEN = 512
NORM_EPS = 1e-6
NEG_INF = -1e30
EXP_CLAMP = 80.0

V7X_VMEM_LIMIT_BYTES = 56 * 1024 * 1024
TOKEN_TILE = 256
MOE_TILE = 256


def _cparams(sem):
    return pltpu.CompilerParams(dimension_semantics=sem, vmem_limit_bytes=V7X_VMEM_LIMIT_BYTES)


def _dot(a, b):
    return jnp.dot(a, b, preferred_element_type=F32)


def _dot_nt(a, b):
    return lax.dot_general(a, b, (((1,), (1,)), ((), ())), preferred_element_type=F32)


def _dot_tn(a, b):
    return lax.dot_general(a, b, (((0,), (0,)), ((), ())), preferred_element_type=F32)


def _split(a):
    hi = a.astype(BF16)
    lo = (a - hi.astype(F32)).astype(BF16)
    return hi, lo


def _dot_f32(a, b):
    ah, al = _split(a)
    bh, bl = _split(b)
    return _dot(ah, bh) + _dot(al, bh) + _dot(ah, bl)


def _sigmoid(x):
    return 1.0 / (1.0 + jnp.exp(-x))


def _silu(x):
    return x * _sigmoid(x)


def _mod_kernel(c_ref, w_ref, b_ref, o_ref):
    c = c_ref[...]
    o_ref[0] = _dot_f32(_silu(c), w_ref[0]) + b_ref[0]


def _mods(c_all, mod_w, mod_b):
    depth, d, d6 = mod_w.shape
    r = c_all.shape[0]
    bn = 1024
    return pl.pallas_call(
        _mod_kernel,
        grid=(depth, d6 // bn),
        in_specs=[pl.BlockSpec((r, d), lambda l, j: (0, 0)),
                  pl.BlockSpec((1, d, bn), lambda l, j: (l, 0, j)),
                  pl.BlockSpec((1, 1, bn), lambda l, j: (l, 0, j))],
        out_specs=pl.BlockSpec((1, r, bn), lambda l, j: (l, 0, j)),
        out_shape=jax.ShapeDtypeStruct((depth, r, d6), F32),
        compiler_params=_cparams(("arbitrary", "arbitrary")),
        name="mods",
    )(c_all, mod_w, mod_b.reshape(depth, 1, d6))


def _proj_kernel(x_ref, ms_ref, nw_ref, w_ref, cs_ref, sn_ref, qa_ref, qb_ref, hg_ref, u_ref, ub_ref):
    x = x_ref[0]
    var = jnp.mean(x * x, axis=-1, keepdims=True)
    y = x * lax.rsqrt(var + NORM_EPS) * nw_ref[...]
    shift = ms_ref[0, 0, 0:1, :]
    scale = ms_ref[0, 0, 1:2, :]
    h = (y * (1.0 + scale) + shift).astype(BF16)
    r = _dot(h, w_ref[:, 0:384])
    tm = r.shape[0]
    first_half = (lax.broadcasted_iota(jnp.int32, (tm, 128), 1) % 32) < 16
    for t in range(3):
        xt = r[:, 128 * t:128 * (t + 1)]
        partner = jnp.where(first_half, pltpu.roll(xt, 128 - 16, 1), pltpu.roll(xt, 16, 1))
        roped = xt * cs_ref[:, 128 * t:128 * (t + 1)] + partner * sn_ref[:, 128 * t:128 * (t + 1)]
        qa_ref[0, :, 128 * t:128 * (t + 1)] = roped.astype(BF16)
        if t == 2:
            qa_ref[0, :, 384:512] = pltpu.roll(roped, HEAD_DIM, 1).astype(BF16)
    t = _dot(h, w_ref[:, 384:1280])
    qa_ref[0, :, 512:640] = t[:, 0:128].astype(BF16)
    qa_ref[0, :, 640:768] = pltpu.roll(t[:, 0:128], HEAD_DIM, 1).astype(BF16)
    qb_ref[0, :, 0:256] = (t[:, 128:384] * (HEAD_DIM ** -0.5)).astype(BF16)
    qb_ref[0, :, 256:768] = t[:, 384:896].astype(BF16)
    hg_ref[0] = _dot(h, w_ref[:, 1280:2560])
    u = _dot(h, w_ref[:, 2560:2816])
    u_ref[0] = u
    ub_ref[0] = u.astype(BF16)


def _proj(x_all, msel, nw, w_ext, cs, sn, lc, tm):
    b, s, d = x_all.shape
    nt = s // tm
    seg = lambda j: jnp.where(j * tm >= lc, 1, 0)
    return pl.pallas_call(
        _proj_kernel,
        grid=(b, nt),
        in_specs=[pl.BlockSpec((1, tm, d), lambda i, j: (i, j, 0)),
                  pl.BlockSpec((1, 1, 6, d), lambda i, j: (i, seg(j), 0, 0)),
                  pl.BlockSpec((1, d), lambda i, j: (0, 0)),
                  pl.BlockSpec((d, 2816), lambda i, j: (0, 0)),
                  pl.BlockSpec((tm, 384), lambda i, j: (j, 0)),
                  pl.BlockSpec((tm, 384), lambda i, j: (j, 0))],
        out_specs=[pl.BlockSpec((1, tm, 768), lambda i, j: (i, j, 0)),
                   pl.BlockSpec((1, tm, 768), lambda i, j: (i, j, 0)),
                   pl.BlockSpec((1, tm, 1280), lambda i, j: (i, j, 0)),
                   pl.BlockSpec((1, tm, 256), lambda i, j: (i, j, 0)),
                   pl.BlockSpec((1, tm, 256), lambda i, j: (i, j, 0))],
        out_shape=[jax.ShapeDtypeStruct((b, s, 768), BF16),
                   jax.ShapeDtypeStruct((b, s, 768), BF16),
                   jax.ShapeDtypeStruct((b, s, 1280), F32),
                   jax.ShapeDtypeStruct((b, s, 256), F32),
                   jax.ShapeDtypeStruct((b, s, 256), BF16)],
        compiler_params=_cparams(("arbitrary", "arbitrary")),
        name="proj",
    )(x_all, msel, nw, w_ext, cs, sn)


def _attend(q, parts, sink):
    scores = []
    for k, _, mask in parts:
        s = _dot_nt(q, k)
        if mask is not None:
            s = s + mask
        scores.append(s)
    m = functools.reduce(jnp.maximum, [jnp.max(s, axis=-1, keepdims=True) for s in scores])
    if sink is not None:
        m = jnp.maximum(m, sink)
    den = jnp.zeros_like(m)
    out = None
    for s, (_, v, _) in zip(scores, parts):
        p = jnp.exp(s - m)
        den = den + jnp.sum(p, axis=-1, keepdims=True)
        o = _dot(p.astype(BF16), v)
        out = o if out is None else out + o
    if sink is not None:
        den = den + jnp.exp(sink - m)
    return out / den


def _swa_kernel(sink_ref, qa_ref, o_ref, *, lc, n, tq, nct):
    j = pl.program_id(1)
    win = tq + 2 * SWA_WINDOW
    lo = lax.broadcasted_iota(jnp.int32, (tq, 128), 1) < HEAD_DIM
    row_lo = lax.broadcasted_iota(jnp.int32, (2 * tq, 1), 0) < tq
    sink_a = jnp.where(row_lo, sink_ref[0], sink_ref[3])
    sink_b = jnp.where(row_lo, sink_ref[1], sink_ref[2])
    ctx = [qa_ref[0, 0:lc, 128 * t:128 * (t + 1)] for t in range(2, 6)]

    def heads(q, lat):
        zero = jnp.zeros_like(q[:, 0:128])
        q01, q23 = q[:, 0:128], q[:, 128:256]
        qa = jnp.concatenate([jnp.where(lo, q01, zero), jnp.where(lo, zero, q23)], axis=0)
        qb = jnp.concatenate([jnp.where(lo, zero, q01), jnp.where(lo, q23, zero)], axis=0)
        pa, pb = [], []
        if lat is not None:
            kw, kx, vw, vx, mask = lat
            mask2 = jnp.concatenate([mask, mask], axis=0)
            pa.append((kw, vw, mask2))
            pb.append((kx, vx, mask2))
        pa.append((ctx[0], ctx[2], None))
        pb.append((ctx[1], ctx[3], None))
        oa = _attend(qa, pa, sink_a)
        ob = _attend(qb, pb, sink_b)
        out01 = jnp.where(lo, oa[0:tq], ob[0:tq])
        out23 = jnp.where(lo, ob[tq:2 * tq], oa[tq:2 * tq])
        return jnp.concatenate([out01, out23], axis=-1).astype(BF16)

    if nct > 0:
        @pl.when(j < nct)
        def _():
            r0 = pl.multiple_of(j * tq, tq)
            o_ref[0] = heads(qa_ref[0, pl.ds(r0, tq), 0:256], None)

    @pl.when(j >= nct)
    def _():
        i = j - nct
        q0 = pl.multiple_of(lc + i * tq, 16)
        ks = jnp.clip(i * tq - SWA_WINDOW, 0, n - win)
        k0 = pl.multiple_of(lc + ks, 16)
        q = qa_ref[0, pl.ds(q0, tq), 0:256]
        lat = [qa_ref[0, pl.ds(k0, win), 128 * t:128 * (t + 1)] for t in range(2, 6)]
        qpos = i * tq + lax.broadcasted_iota(jnp.int32, (tq, win), 0)
        kpos = ks + lax.broadcasted_iota(jnp.int32, (tq, win), 1)
        mask = jnp.where(jnp.abs(qpos - kpos) <= SWA_WINDOW, 0.0, NEG_INF).astype(F32)
        o_ref[0] = heads(q, lat + [mask])


def _swa(qa, sink, lc, need_ctx):
    b, s, _ = qa.shape
    n = s - lc
    tq = min(128, lc)
    nct = lc // tq if need_ctx else 0
    return pl.pallas_call(
        functools.partial(_swa_kernel, lc=lc, n=n, tq=tq, nct=nct),
        grid=(b, nct + n // tq),
        in_specs=[pl.BlockSpec(memory_space=pltpu.SMEM),
                  pl.BlockSpec((1, s, 768), lambda i, j: (i, 0, 0))],
        out_specs=pl.BlockSpec((1, tq, 256), lambda i, j: (i, j, 0)),
        out_shape=jax.ShapeDtypeStruct((b, nct * tq + n, 256), BF16),
        compiler_params=_cparams(("arbitrary", "arbitrary")),
        name="swa",
    )(sink, qa)


def _na_kernel(qb_ref, *rest, lc, rows, nct, per_step):
    bias_refs, o_ref = rest[:per_step], rest[per_step]
    j = pl.program_id(1)
    nk = NA_ROWS * GRID_W
    kc = qb_ref[0, 0:lc, 256:512]
    vc = qb_ref[0, 0:lc, 512:768]

    m = GRID_W
    rb = lax.broadcasted_iota(jnp.int32, (4 * m, 256), 0) // m
    lb = lax.broadcasted_iota(jnp.int32, (4 * m, 256), 1) // HEAD_DIM
    own = rb == lb

    def heads(q, lat):
        qbd = jnp.where(own, jnp.concatenate([q] * 4, axis=0), jnp.zeros((4 * m, 256), BF16))
        parts = []
        if lat is not None:
            kw, vw, bias = lat
            parts.append((kw, vw, bias))
        parts.append((kc, vc, None))
        o4 = jnp.where(own, _attend(qbd, parts, None), 0.0)
        return (o4[0:m] + o4[m:2 * m] + o4[2 * m:3 * m] + o4[3 * m:4 * m]).astype(BF16)

    if nct > 0:
        @pl.when(j < nct)
        def _():
            for t in range(per_step):
                r0 = pl.multiple_of((j * per_step + t) * m, m)
                o_ref[0, m * t:m * (t + 1), :] = heads(qb_ref[0, pl.ds(r0, m), 0:256], None)

    @pl.when(j >= nct)
    def _():
        for t in range(per_step):
            r = (j - nct) * per_step + t
            rs = jnp.clip(r - NA_ROWS // 2, 0, rows - NA_ROWS)
            q0 = pl.multiple_of(lc + r * m, m)
            k0 = pl.multiple_of(lc + rs * m, m)
            q = qb_ref[0, pl.ds(q0, m), 0:256]
            kw = qb_ref[0, pl.ds(k0, nk), 256:512]
            vw = qb_ref[0, pl.ds(k0, nk), 512:768]
            o_ref[0, m * t:m * (t + 1), :] = heads(q, (kw, vw, bias_refs[t][0]))


def _na(qb, bias, lc, need_ctx):
    b, s, _ = qb.shape
    n = s - lc
    rows = n // GRID_W
    ctx_tiles = lc // GRID_W if need_ctx else 0
    per_step = 2 if (ctx_tiles % 2 == 0 and rows % 2 == 0) else 1
    nct = ctx_tiles // per_step

    def cls(t):
        def index(i, j):
            r = jnp.maximum((j - nct) * per_step + t, 0)
            return (r - jnp.clip(r - NA_ROWS // 2, 0, rows - NA_ROWS), 0, 0)
        return index

    return pl.pallas_call(
        functools.partial(_na_kernel, lc=lc, rows=rows, nct=nct, per_step=per_step),
        grid=(b, nct + rows // per_step),
        in_specs=[pl.BlockSpec((1, s, 768), lambda i, j: (i, 0, 0))]
                 + [pl.BlockSpec((1, 4 * GRID_W, NA_ROWS * GRID_W), cls(t)) for t in range(per_step)],
        out_specs=pl.BlockSpec((1, per_step * GRID_W, 256), lambda i, j: (i, j, 0)),
        out_shape=jax.ShapeDtypeStruct((b, ctx_tiles * GRID_W + n, 256), BF16),
        compiler_params=_cparams(("arbitrary", "arbitrary")),
        name="na",
    )(qb, *([bias] * per_step))


def _na_bias(rpb):
    c = np.arange(GRID_W)[:, None]
    kc = np.arange(GRID_W)[None, :]
    ci = np.clip(kc - c + NA_COLS - 1, 0, 2 * NA_COLS - 2)
    ws = np.clip(c - NA_COLS // 2, 0, GRID_W - NA_COLS)
    valid = (kc >= ws) & (kc < ws + NA_COLS)
    sel = (ci[None] == np.arange(2 * NA_COLS - 1)[:, None, None]).astype(np.float32)
    t = jnp.einsum('hrd,dck->hrck', rpb.astype(F32), sel, precision=lax.Precision.HIGHEST)
    t = jnp.where(valid[None, None], t, NEG_INF)
    per_cls = [t[:, NA_ROWS - 1 - d:2 * NA_ROWS - 1 - d] for d in range(NA_ROWS)]
    t = jnp.transpose(jnp.stack(per_cls, axis=0), (0, 1, 3, 2, 4))
    return t.reshape(NA_ROWS, 4 * GRID_W, NA_ROWS * GRID_W)


def _hgrn_dir(blk, lb, st_ref, rev):
    c = HGRN_CHUNK
    sub = HGRN_SUB
    nsub = c // sub
    q = _silu(blk[:, 0:256])
    fraw = blk[:, 512:768] if rev else blk[:, 256:512]
    v = blk[:, 768:1024]
    f = lb + (1.0 - lb) * _sigmoid(fraw)
    kk = 1.0 - f
    logf = jnp.log(f)
    ti = lax.broadcasted_iota(jnp.int32, (c, c), 0)
    si = lax.broadcasted_iota(jnp.int32, (c, c), 1)
    tri = jnp.where((si >= ti) if rev else (si <= ti), 1.0, 0.0).astype(BF16)
    bcum = _dot_exact_rhs_left(tri, logf)
    btot = bcum[0:1] if rev else bcum[c - 1:c]
    vb = v.astype(BF16)
    st = st_ref[...]
    o_inter = _dot_nt((q * jnp.exp(bcum)).astype(BF16), st.astype(BF16))

    rh = lax.broadcasted_iota(jnp.int32, (c, 256), 0) // sub
    lh = lax.broadcasted_iota(jnp.int32, (c, 256), 1) // HEAD_DIM
    bd = jnp.where(rh == lh, 1.0, 0.0).astype(F32)
    outs = []
    for i in range(nsub):
        r0, r1 = i * sub, (i + 1) * sub
        if rev:
            rho = bcum[r1:r1 + 1] if i < nsub - 1 else jnp.zeros((1, 256), F32)
            k0, k1 = r0, c
        else:
            rho = bcum[r0 - 1:r0] if i > 0 else jnp.zeros((1, 256), F32)
            k0, k1 = 0, r1
        qh = q[r0:r1] * jnp.exp(bcum[r0:r1] - rho)
        kh = kk[k0:k1] * jnp.exp(jnp.minimum(rho - bcum[k0:k1], EXP_CLAMP))
        qbd = (jnp.concatenate([qh] * 4, axis=0) * bd).astype(BF16)
        a = _dot_nt(qbd, kh.astype(BF16))
        nk = k1 - k0
        tq = r0 + lax.broadcasted_iota(jnp.int32, (c, nk), 0) % sub
        sk = k0 + lax.broadcasted_iota(jnp.int32, (c, nk), 1)
        a = jnp.where((sk >= tq) if rev else (sk <= tq), a, 0.0)
        o4 = _dot(a.astype(BF16), vb[k0:k1]) * bd
        outs.append(o4[0:sub] + o4[sub:2 * sub] + o4[2 * sub:3 * sub] + o4[3 * sub:4 * sub])
    o = o_inter + jnp.concatenate(outs, axis=0)

    kend = (kk * jnp.exp(btot - bcum)).astype(BF16)
    r2 = lax.broadcasted_iota(jnp.int32, (256, 256), 0) // HEAD_DIM
    l2 = lax.broadcasted_iota(jnp.int32, (256, 256), 1) // HEAD_DIM
    upd = jnp.where(r2 == l2, _dot_tn(vb, kend), 0.0)
    st_ref[...] = st * jnp.exp(btot) + upd
    return o


def _dot_exact_rhs_left(tri_bf16, a):
    a0, a1 = _split(a)
    return _dot(tri_bf16, a0) + _dot(tri_bf16, a1)


HGRN_BATCH = 2


def _hgrn_kernel(hf_ref, hb_ref, lb_ref, of_ref, ob_ref, stf_ref, stb_ref):
    @pl.when(pl.program_id(1) == 0)
    def _():
        stf_ref[...] = jnp.zeros_like(stf_ref)
        stb_ref[...] = jnp.zeros_like(stb_ref)

    for bi in range(HGRN_BATCH):
        of_ref[bi] = _hgrn_dir(hf_ref[bi], lb_ref[0:1, :], stf_ref.at[bi], False)
        ob_ref[bi] = _hgrn_dir(hb_ref[bi], lb_ref[1:2, :], stb_ref.at[bi], True)


def _hgrn(hg, lb2, lc):
    b, s, _ = hg.shape
    c = HGRN_CHUNK
    nc = s // c
    ncc = lc // c

    def bwd(j):
        return jnp.where(j < ncc, ncc - 1 - j, ncc + nc - 1 - j)

    hb = HGRN_BATCH
    return pl.pallas_call(
        _hgrn_kernel,
        grid=(b // hb, nc),
        in_specs=[pl.BlockSpec((hb, c, 1280), lambda i, j: (i, j, 0)),
                  pl.BlockSpec((hb, c, 1280), lambda i, j: (i, bwd(j), 0)),
                  pl.BlockSpec((2, 256), lambda i, j: (0, 0))],
        out_specs=[pl.BlockSpec((hb, c, 256), lambda i, j: (i, j, 0)),
                   pl.BlockSpec((hb, c, 256), lambda i, j: (i, bwd(j), 0))],
        out_shape=[jax.ShapeDtypeStruct((b, s, 256), F32),
                   jax.ShapeDtypeStruct((b, s, 256), F32)],
        scratch_shapes=[pltpu.VMEM((hb, 256, 256), F32), pltpu.VMEM((hb, 256, 256), F32)],
        compiler_params=_cparams(("arbitrary", "arbitrary")),
        name="hgrn",
    )(hg, hg, lb2)


def _s5_kernel(u_ref, k_ref, bm_ref, cm_ref, at_ref, y_ref,
               hin_ref, hst_ref, *, bsz, ncc, nc):
    ub = u_ref[0]
    for m in range(4):
        hin_ref[m] = _dot(ub, bm_ref[0, m])

    def run(direction, order_fn, count, h0):
        a_re = at_ref[0, 2 * direction:2 * direction + 1, :]
        a_im = at_ref[0, 2 * direction + 1:2 * direction + 2, :]

        def body(t, carry):
            h_re, h_im = carry
            r0 = pl.multiple_of(order_fn(t) * bsz, 8)
            hst_ref[2 * direction, pl.ds(r0, bsz), :] = h_re
            hst_ref[2 * direction + 1, pl.ds(r0, bsz), :] = h_im
            n_re = a_re * h_re - a_im * h_im + hin_ref[2 * direction, pl.ds(r0, bsz), :]
            n_im = a_re * h_im + a_im * h_re + hin_ref[2 * direction + 1, pl.ds(r0, bsz), :]
            return n_re, n_im

        return lax.fori_loop(0, count, body, h0)

    z = jnp.zeros((bsz, S5_STATE), F32)
    run(0, lambda t: t, nc, (z, z))
    hb = run(1, lambda t: ncc - 1 - t, ncc, (z, z))
    run(1, lambda t: nc - 1 - t, nc - ncc, hb)

    y = _dot(ub, k_ref[0])
    for m in range(4):
        y = y + _dot(hst_ref[m].astype(BF16), cm_ref[0, m])
    y_ref[0] = y.astype(BF16)


def _s5(ug, kmat, bmat, cmat, at, bsz, ncc):
    g, rows, _ = ug.shape
    nc = rows // bsz
    return pl.pallas_call(
        functools.partial(_s5_kernel, bsz=bsz, ncc=ncc, nc=nc),
        grid=(g,),
        in_specs=[pl.BlockSpec((1, rows, 256), lambda i: (i, 0, 0)),
                  pl.BlockSpec((1, 256, 256), lambda i: (i, 0, 0)),
                  pl.BlockSpec((1, 4, 256, S5_STATE), lambda i: (i, 0, 0, 0)),
                  pl.BlockSpec((1, 4, S5_STATE, 256), lambda i: (i, 0, 0, 0)),
                  pl.BlockSpec((1, 4, S5_STATE), lambda i: (i, 0, 0))],
        out_specs=pl.BlockSpec((1, rows, 256), lambda i: (i, 0, 0)),
        out_shape=jax.ShapeDtypeStruct((g, rows, 256), BF16),
        scratch_shapes=[pltpu.VMEM((4, rows, S5_STATE), F32), pltpu.VMEM((4, rows, S5_STATE), F32)],
        compiler_params=_cparams(("arbitrary",)),
        name="s5",
    )(ug, kmat, bmat, cmat, at)


def _s5_matrices(lam_re, lam_im, log_dt, b_re, b_im, c_re, c_im):
    t = S5_CHUNK
    hp = lax.Precision.HIGHEST
    lam_re, lam_im, log_dt = lam_re.astype(F32), lam_im.astype(F32), log_dt.astype(F32)
    b_re, b_im, c_re, c_im = b_re.astype(F32), b_im.astype(F32), c_re.astype(F32), c_im.astype(F32)
    dt = jnp.exp(log_dt)[..., None]
    jj = jnp.arange(t + 1, dtype=F32)[:, None, None, None]
    mag = jnp.exp(lam_re * dt * jj)
    ang = lam_im * dt * jj
    aj_re, aj_im = mag * jnp.cos(ang), mag * jnp.sin(ang)
    a_re, a_im = aj_re[1], aj_im[1]
    den = lam_re * lam_re + lam_im * lam_im
    k_re = ((a_re - 1.0) * lam_re + a_im * lam_im) / den
    k_im = (a_im * lam_re - (a_re - 1.0) * lam_im) / den
    bb_re = k_re[..., None] * b_re - k_im[..., None] * b_im
    bb_im = k_re[..., None] * b_im + k_im[..., None] * b_re
    ab_re = aj_re[..., None] * bb_re - aj_im[..., None] * bb_im
    ab_im = aj_re[..., None] * bb_im + aj_im[..., None] * bb_re
    kj = (jnp.einsum('dghp,jdgpi->jdghi', c_re, ab_re, precision=hp)
          - jnp.einsum('dghp,jdgpi->jdghi', c_im, ab_im, precision=hp))
    s_i = np.arange(t)[:, None]
    t_i = np.arange(t)[None, :]
    kf = jnp.where((t_i >= s_i)[:, :, None, None, None], kj[np.clip(t_i - s_i, 0, t), 0], 0.0)
    kb = jnp.where((s_i >= t_i)[:, :, None, None, None], kj[np.clip(s_i - t_i, 0, t), 1], 0.0)
    kmat = jnp.transpose(kf + kb, (2, 0, 4, 1, 3)).reshape(S5_GROUPS, 256, 256)

    def state_in(ab, idx, direction):
        return jnp.transpose(ab[idx, direction], (1, 0, 3, 2)).reshape(S5_GROUPS, 256, S5_STATE)

    fw_idx = np.arange(t - 1, -1, -1)
    bw_idx = np.arange(t)
    bmat = jnp.stack([state_in(ab_re, fw_idx, 0), state_in(ab_im, fw_idx, 0),
                      state_in(ab_re, bw_idx, 1), state_in(ab_im, bw_idx, 1)], axis=1)

    def state_out(idx, direction):
        ar = aj_re[idx, direction][:, :, None, :]
        ai = aj_im[idx, direction][:, :, None, :]
        cr, ci = c_re[direction][None], c_im[direction][None]
        re = cr * ar - ci * ai
        im = -(cr * ai + ci * ar)
        f = lambda m: jnp.transpose(m, (1, 3, 0, 2)).reshape(S5_GROUPS, S5_STATE, 256)
        return f(re), f(im)

    cf_re, cf_im = state_out(np.arange(1, t + 1), 0)
    cb_re, cb_im = state_out(np.arange(t, 0, -1), 1)
    cmat = jnp.stack([cf_re, cf_im, cb_re, cb_im], axis=1)
    at = jnp.stack([aj_re[t, 0], aj_im[t, 0], aj_re[t, 1], aj_im[t, 1]], axis=1)
    return kmat.astype(BF16), bmat.astype(BF16), cmat.astype(BF16), at


def _gelu_tanh(x):
    return 0.5 * x * (1.0 + jnp.tanh(math.sqrt(2.0 / math.pi) * (x + 0.044715 * (x * x * x))))


def _store_token_tiles(ref, val):
    m = val.shape[0]
    for k in range(8):
        ref[pl.ds(k, m, stride=8), :] = val[:, 128 * k:128 * (k + 1)]


def _load_token_tiles(ref, m):
    return jnp.concatenate([ref[pl.ds(k, m, stride=8), :] for k in range(8)], axis=-1)


def _outproj_kernel(a_ref, b_ref, of_ref, ob_ref, g_ref, y5_ref, u_ref, d_ref, x_ref, ms_ref, hnw_ref,
                    gw_ref, gb_ref, wo_ref, n2_ref, rw_ref, rb_ref,
                    xo_ref, h2_ref, rt_ref, cnt_ref, run_ref):
    @pl.when((pl.program_id(0) == 0) & (pl.program_id(1) == 0))
    def _():
        run_ref[...] = jnp.zeros_like(run_ref)

    o = of_ref[0] + ob_ref[0]
    r2 = lax.broadcasted_iota(jnp.int32, (256, 256), 0) // HEAD_DIM
    l2 = lax.broadcasted_iota(jnp.int32, (256, 256), 1) // HEAD_DIM
    hm = jnp.where(r2 == l2, 1.0, 0.0).astype(BF16)
    sq = o * o
    sh, sl = _split(sq)
    ms = (_dot(sh, hm) + _dot(sl, hm)) * (1.0 / HEAD_DIM)
    cm = o * lax.rsqrt(ms + NORM_EPS) * hnw_ref[...] * _silu(g_ref[0])
    z = _gelu_tanh(y5_ref[0].astype(F32) + d_ref[...] * u_ref[0])
    dm = z * _sigmoid(_dot(z.astype(BF16), gw_ref[...]) + gb_ref[...])
    mix = (_dot(a_ref[0], wo_ref[0:256, :]) + _dot(b_ref[0], wo_ref[256:512, :])
           + _dot(cm.astype(BF16), wo_ref[512:768, :]) + _dot(dm.astype(BF16), wo_ref[768:1024, :]))
    x = x_ref[0] + ms_ref[0, 0, 2:3, :] * mix
    xo_ref[0] = x
    var = jnp.mean(x * x, axis=-1, keepdims=True)
    h2 = x * lax.rsqrt(var + NORM_EPS) * n2_ref[...]
    h2 = h2 * (1.0 + ms_ref[0, 0, 4:5, :]) + ms_ref[0, 0, 3:4, :]
    _store_token_tiles(h2_ref, h2)
    lg = _dot_f32(h2, rw_ref[...]) + rb_ref[...]
    tm = lg.shape[0]
    lane = lax.broadcasted_iota(jnp.int32, (tm, 128), 1)
    big = jnp.int32(1 << 20)
    ninf = jnp.float32(-jnp.inf)
    gmask = lane < MOE_GROUPS
    gl = jnp.where(gmask, lg, ninf)
    gmax = jnp.max(gl, axis=-1, keepdims=True)
    gidx = jnp.min(jnp.where(gl == gmax, lane, big), axis=-1, keepdims=True)
    gsum = jnp.sum(jnp.where(gmask, jnp.exp(gl - gmax), 0.0), axis=-1, keepdims=True)
    gwt = 1.0 / gsum
    e = lane - MOE_GROUPS
    emask = (e >= 0) & (e < MOE_EXPERTS) & (lax.shift_right_arithmetic(e, 3) == gidx)
    el = jnp.where(emask, lg, ninf)
    v1 = jnp.max(el, axis=-1, keepdims=True)
    i1 = jnp.min(jnp.where(el == v1, lane, big), axis=-1, keepdims=True)
    el2 = jnp.where(lane == i1, ninf, el)
    v2 = jnp.max(el2, axis=-1, keepdims=True)
    i2 = jnp.min(jnp.where(el2 == v2, lane, big), axis=-1, keepdims=True)
    t2 = jnp.exp(v2 - v1)
    w1 = gwt / (1.0 + t2)
    w2 = w1 * t2
    pick1 = lane == i1
    pick2 = lane == i2
    oh = jnp.where(pick1 | pick2, 1.0, 0.0).astype(F32)
    ti = lax.broadcasted_iota(jnp.int32, (tm, tm), 0)
    si = lax.broadcasted_iota(jnp.int32, (tm, tm), 1)
    before = _dot(jnp.where(si < ti, 1.0, 0.0).astype(BF16), oh.astype(BF16)) + run_ref[0:1, :]
    r1 = jnp.sum(jnp.where(pick1, before, 0.0), axis=-1, keepdims=True)
    r2 = jnp.sum(jnp.where(pick2, before, 0.0), axis=-1, keepdims=True)
    run = run_ref[...] + jnp.sum(oh, axis=0, keepdims=True)
    run_ref[...] = run
    cnt_ref[...] = run
    vals = ((i1 - MOE_GROUPS).astype(F32), (i2 - MOE_GROUPS).astype(F32), w1, w2, r1, r2)
    rt = jnp.zeros((tm, 128), F32)
    for k, val in enumerate(vals):
        rt = jnp.where(lane == k, val, rt)
    rt_ref[0] = rt


def _outproj(a_out, b_out, o_f, o_b, hg, y5, u, dvec, x_all, msel, hnw, glu_w, glu_b, w_out, n2w, rw, rb,
             lc, tm, need_ctx):
    b, s, d = x_all.shape
    off = 0 if need_ctx else lc // tm
    nt = s // tm - off
    rows = nt * tm
    seg = lambda j: jnp.where((j + off) * tm >= lc, 1, 0)
    tok = lambda w: pl.BlockSpec((1, tm, w), lambda i, j: (i, j + off, 0))
    loc = lambda w: pl.BlockSpec((1, tm, w), lambda i, j: (i, j, 0))
    full = lambda r, c: pl.BlockSpec((r, c), lambda i, j: (0, 0))
    return pl.pallas_call(
        _outproj_kernel,
        grid=(b, nt),
        in_specs=[loc(256), loc(256), tok(256), tok(256),
                  pl.BlockSpec((1, tm, 256), lambda i, j: (i, j + off, 4)),
                  tok(256), tok(256), full(1, 256), tok(d),
                  pl.BlockSpec((1, 1, 6, d), lambda i, j: (i, seg(j), 0, 0)),
                  full(1, 256), full(256, 256), full(1, 256), full(d, d), full(1, d),
                  full(d, 128), full(1, 128)],
        out_specs=[loc(d), pl.BlockSpec((tm * 8, 128), lambda i, j: (i * nt + j, 0)), loc(128),
                   full(8, 128)],
        out_shape=[jax.ShapeDtypeStruct((b, rows, d), F32),
                   jax.ShapeDtypeStruct((b * rows * 8, 128), F32),
                   jax.ShapeDtypeStruct((b, rows, 128), F32),
                   jax.ShapeDtypeStruct((8, 128), F32)],
        scratch_shapes=[pltpu.VMEM((8, 128), F32)],
        compiler_params=_cparams(("arbitrary", "arbitrary")),
        name="outproj",
    )(a_out, b_out, o_f, o_b, hg, y5, u, dvec, x_all, msel, hnw, glu_w, glu_b, w_out, n2w, rw, rb)


DMA_UNROLL = 8


def _dispatch_kernel(slot_ref, h_ref, xs_in, xs_out, sem, *, tm):
    del xs_in

    def tok_copy(r, k, dst):
        src = pl.multiple_of(r * 8, 8)
        return pltpu.make_async_copy(h_ref.at[pl.ds(src, 8)], xs_out.at[pl.ds(pl.multiple_of(dst, 8), 8)],
                                     sem.at[k])

    def issue(r, c):
        tok_copy(r, 0, slot_ref[0, 0, r]).start()
        tok_copy(r, 1, slot_ref[0, 1, r]).start()
        return c

    lax.fori_loop(0, tm, issue, 0, unroll=DMA_UNROLL)
    for k in range(2):
        pltpu.make_async_copy(h_ref, xs_out.at[pl.ds(0, tm * 8)], sem.at[k]).wait()


def _dispatch(h2_tiles, slots3, n_slots, tm):
    rows8 = h2_tiles.shape[0]
    xs0 = jnp.zeros((n_slots * 8, 128), F32)
    return pl.pallas_call(
        functools.partial(_dispatch_kernel, tm=tm),
        grid=(rows8 // (tm * 8),),
        in_specs=[pl.BlockSpec((1, 2, tm), lambda i: (i, 0, 0), memory_space=pltpu.SMEM),
                  pl.BlockSpec((tm * 8, 128), lambda i: (i, 0)),
                  pl.BlockSpec(memory_space=pl.ANY)],
        out_specs=pl.BlockSpec(memory_space=pl.ANY),
        out_shape=jax.ShapeDtypeStruct((n_slots * 8, 128), F32),
        scratch_shapes=[pltpu.SemaphoreType.DMA((2,))],
        input_output_aliases={2: 0},
        compiler_params=_cparams(("arbitrary",)),
        name="dispatch",
    )(slots3, h2_tiles, xs0)


def _moe_kernel(te_ref, nact_ref, x_ref, wg_ref, wu_ref, wd_ref, y_ref, wgb, wub, wdb, *, tm):
    i = pl.program_id(0)

    @pl.when((i == 0) | (te_ref[i] != te_ref[jnp.maximum(i - 1, 0)]))
    def _():
        wgb[...] = wg_ref[0, 0].astype(BF16)
        wub[...] = wu_ref[0, 0].astype(BF16)
        wdb[...] = wd_ref[0, 0].astype(BF16)

    @pl.when(i < nact_ref[0])
    def _():
        gate = None
        up = None
        for p in range(4):
            xp = jnp.concatenate([x_ref[pl.ds(2 * p, tm, stride=8), :],
                                  x_ref[pl.ds(2 * p + 1, tm, stride=8), :]], axis=-1).astype(BF16)
            g = _dot(xp, wgb[256 * p:256 * (p + 1), :])
            u = _dot(xp, wub[256 * p:256 * (p + 1), :])
            gate = g if gate is None else gate + g
            up = u if up is None else up + u
        hid = (_silu(gate) * up).astype(BF16)
        _store_token_tiles(y_ref, _dot(hid, wdb[...]))

    @pl.when(i >= nact_ref[0])
    def _():
        y_ref[...] = jnp.zeros_like(y_ref)


def _moe(xs, wg, wu, wd, layer, tile_expert, nact, tm):
    n_slots = xs.shape[0] // 8
    _, _, d, hid = wg.shape
    nt = n_slots // tm
    return pl.pallas_call(
        functools.partial(_moe_kernel, tm=tm),
        grid_spec=pltpu.PrefetchScalarGridSpec(
            num_scalar_prefetch=2,
            grid=(nt,),
            in_specs=[pl.BlockSpec((tm * 8, 128), lambda i, te, na: (i, 0)),
                      pl.BlockSpec((1, 1, d, hid), lambda i, te, na: (layer, te[i], 0, 0)),
                      pl.BlockSpec((1, 1, d, hid), lambda i, te, na: (layer, te[i], 0, 0)),
                      pl.BlockSpec((1, 1, hid, d), lambda i, te, na: (layer, te[i], 0, 0))],
            out_specs=pl.BlockSpec((tm * 8, 128), lambda i, te, na: (i, 0)),
            scratch_shapes=[pltpu.VMEM((d, hid), BF16), pltpu.VMEM((d, hid), BF16),
                            pltpu.VMEM((hid, d), BF16)]),
        out_shape=jax.ShapeDtypeStruct((n_slots * 8, 128), F32),
        compiler_params=_cparams(("arbitrary",)),
        name="moe",
    )(tile_expert, nact, xs, wg, wu, wd)


def _dispatch_plan(route, counts, tok_tile, moe_tile):
    t = route.shape[0]
    eid = route[:, 0:2].astype(jnp.int32)
    rank = route[:, 4:6].astype(jnp.int32)
    padded = ((counts + moe_tile - 1) // moe_tile) * moe_tile
    pad_end = jnp.cumsum(padded)
    pad_start = pad_end - padded
    slots = (jnp.take(pad_start, eid, axis=0) + rank) * 8
    slots3 = jnp.transpose(slots.reshape(t // tok_tile, tok_tile, 2), (0, 2, 1))
    nt = (2 * t) // moe_tile + MOE_EXPERTS
    nact = (pad_end[-1] // moe_tile).astype(jnp.int32)
    tile_start = jnp.arange(nt, dtype=jnp.int32) * moe_tile
    te = jnp.sum((pad_end[None, :] <= tile_start[:, None]).astype(jnp.int32), axis=1)
    last_used = jnp.max(jnp.where(counts > 0, jnp.arange(MOE_EXPERTS, dtype=jnp.int32), 0))
    te = jnp.minimum(te, last_used)
    return slots3, te, nact.reshape(1), nt * moe_tile


def _combine_kernel(slot_ref, nslot_ref, x_ref, rt_ref, ms_ref, fw_ref, ys_hbm, o_ref, ybuf, sem,
                    *, final, tm):
    i = pl.program_id(0)
    n = pl.num_programs(0)
    cur = lax.rem(i, 2)

    def tok_copy(src, buf, k, r):
        dst = pl.multiple_of(r * 8, 8)
        return pltpu.make_async_copy(ys_hbm.at[pl.ds(pl.multiple_of(src, 8), 8)],
                                     ybuf.at[buf, k, pl.ds(dst, 8)], sem.at[buf])

    def issue(table, buf):
        def body(r, c):
            tok_copy(table[0, 0, r], buf, 0, r).start()
            tok_copy(table[0, 1, r], buf, 1, r).start()
            return c
        lax.fori_loop(0, tm, body, 0, unroll=DMA_UNROLL)

    @pl.when(i == 0)
    def _():
        issue(slot_ref, 0)

    @pl.when(i + 1 < n)
    def _():
        issue(nslot_ref, 1 - cur)

    for k in range(2):
        pltpu.make_async_copy(ys_hbm.at[pl.ds(0, tm * 8)], ybuf.at[cur, k], sem.at[cur]).wait()

    w0 = rt_ref[:, 2:3]
    w1 = rt_ref[:, 3:4]
    y0 = _load_token_tiles(ybuf.at[cur, 0], tm)
    y1 = _load_token_tiles(ybuf.at[cur, 1], tm)
    x = x_ref[...] + ms_ref[0, 0, 5:6, :] * (w0 * y0 + w1 * y1)
    if final:
        var = jnp.mean(x * x, axis=-1, keepdims=True)
        x = x * lax.rsqrt(var + NORM_EPS) * fw_ref[...]
    o_ref[...] = x


def _combine(x_mid, route, ys, slots3, msel, fw, rows_per_batch, ctx_rows, tm, final):
    rows, d = x_mid.shape
    nt = rows // tm
    ntb = rows_per_batch // tm
    seg = lambda i: jnp.where((i % ntb) * tm >= ctx_rows, 1, 0)
    smem_tile = lambda fn: pl.BlockSpec((1, 2, tm), fn, memory_space=pltpu.SMEM)
    return pl.pallas_call(
        functools.partial(_combine_kernel, final=final, tm=tm),
        grid=(nt,),
        in_specs=[smem_tile(lambda i: (i, 0, 0)),
                  smem_tile(lambda i: (jnp.minimum(i + 1, nt - 1), 0, 0)),
                  pl.BlockSpec((tm, d), lambda i: (i, 0)),
                  pl.BlockSpec((tm, 128), lambda i: (i, 0)),
                  pl.BlockSpec((1, 1, 6, d), lambda i: (i // ntb, seg(i), 0, 0)),
                  pl.BlockSpec((1, d), lambda i: (0, 0)),
                  pl.BlockSpec(memory_space=pl.ANY)],
        out_specs=pl.BlockSpec((tm, d), lambda i: (i, 0)),
        out_shape=jax.ShapeDtypeStruct((rows, d), F32),
        scratch_shapes=[pltpu.VMEM((2, 2, tm * 8, 128), F32), pltpu.SemaphoreType.DMA((2,))],
        compiler_params=_cparams(("arbitrary",)),
        name="combine",
    )(slots3, slots3, x_mid, route, msel, fw, ys)


def _rope_tables(n, lc):
    half = HEAD_DIM // 2
    inv = 1.0 / (ROPE_BASE ** (np.arange(0, half, 2, dtype=np.float64) / half))
    t = np.arange(n)
    ang_r = (t // GRID_W).astype(np.float64)[:, None] * inv[None, :]
    ang_c = (t % GRID_W).astype(np.float64)[:, None] * inv[None, :]
    cos_h = np.concatenate([np.cos(ang_r), np.cos(ang_r), np.cos(ang_c), np.cos(ang_c)], axis=-1)
    sin_h = np.concatenate([-np.sin(ang_r), np.sin(ang_r), -np.sin(ang_c), np.sin(ang_c)], axis=-1)
    cos_h = np.concatenate([np.ones((lc, HEAD_DIM)), cos_h], axis=0)
    sin_h = np.concatenate([np.zeros((lc, HEAD_DIM)), sin_h], axis=0)
    qs = HEAD_DIM ** -0.5
    cs = np.concatenate([np.tile(cos_h, (1, 4)) * qs, np.tile(cos_h, (1, 2))], axis=-1)
    sn = np.concatenate([np.tile(sin_h, (1, 4)) * qs, np.tile(sin_h, (1, 2))], axis=-1)
    return jnp.asarray(cs, F32), jnp.asarray(sn, F32)


def kernel(x, c, ctx, c_ctx, mod_w, mod_b, norm1_w, norm2_w, w_in, w_out, swa_sink, na_rpb, hgrn_lb,
           hgrn_norm_w, s5_lam_re, s5_lam_im, s5_log_dt, s5_b_re, s5_b_im, s5_c_re, s5_c_im, s5_d,
           s5_glu_w, s5_glu_b, moe_group_w, moe_group_b, moe_expert_w, moe_expert_b, moe_w_gate,
           moe_w_up, moe_w_down, final_norm_w):
    bsz, n, d = x.shape
    lc = ctx.shape[1]
    s = lc + n
    depth = mod_w.shape[0]
    tm = min(TOKEN_TILE, lc)
    assert d == 1024 and lc % tm == 0 and n % tm == 0 and lc % HGRN_CHUNK == 0 and bsz % 8 == 0
    assert n % GRID_W == 0 and n // GRID_W >= NA_ROWS and n >= min(128, lc) + 2 * SWA_WINDOW

    x_all = jnp.concatenate([ctx, x], axis=1).astype(F32)

    rpad = (-(bsz + 1)) % 8
    c_all = jnp.concatenate([c, c_ctx[None], jnp.zeros((rpad, d), c.dtype)], axis=0).astype(F32)
    mods = _mods(c_all, mod_w.astype(F32), mod_b.astype(F32))
    mods = mods.reshape(depth, -1, 6, d)

    lbp = jax.nn.softmax(hgrn_lb.astype(F32), axis=1)
    lower = jnp.cumsum(lbp, axis=1) - lbp[:, :1]

    cs, sn = _rope_tables(n, lc)
    nch5 = s // S5_CHUNK

    out = None
    for l in range(depth):
        last = l == depth - 1
        need_ctx = not last
        msel = jnp.stack([jnp.broadcast_to(mods[l, bsz][None], (bsz, 6, d)), mods[l, :bsz]], axis=1)

        qa, qb, hg, u, ub = _proj(x_all, msel, norm1_w[l].reshape(1, d).astype(F32), w_in[l].astype(BF16),
                                  cs, sn, lc, tm)

        a_out = _swa(qa, swa_sink[l].astype(F32), lc, need_ctx)
        b_out = _na(qb, _na_bias(na_rpb[l]), lc, need_ctx)
        o_f, o_b = _hgrn(hg, lower[:, l], lc)

        kmat, bmat, cmat, at = _s5_matrices(
            s5_lam_re[l], s5_lam_im[l], s5_log_dt[l], s5_b_re[l], s5_b_im[l], s5_c_re[l], s5_c_im[l])
        ug = jnp.transpose(ub.reshape(bsz, nch5, S5_CHUNK, S5_GROUPS, S5_GROUP_CH), (3, 1, 0, 2, 4))
        yg = _s5(ug.reshape(S5_GROUPS, nch5 * bsz, 256), kmat, bmat, cmat, at, bsz, lc // S5_CHUNK)
        y5 = jnp.transpose(yg.reshape(S5_GROUPS, nch5, bsz, S5_CHUNK, S5_GROUP_CH), (2, 1, 3, 0, 4))
        y5 = y5.reshape(bsz, s, 256)

        rw = jnp.concatenate([moe_group_w[l], moe_expert_w[l],
                              jnp.zeros((d, 128 - MOE_GROUPS - MOE_EXPERTS), F32)], axis=1).astype(F32)
        rb = jnp.concatenate([moe_group_b[l], moe_expert_b[l],
                              jnp.zeros((128 - MOE_GROUPS - MOE_EXPERTS,), F32)]).reshape(1, 128).astype(F32)
        x_mid, h2, route, cnt = _outproj(
            a_out, b_out, o_f, o_b, hg, y5, u, s5_d[l].reshape(1, 256).astype(F32), x_all, msel,
            jnp.tile(hgrn_norm_w[l].astype(F32), 4).reshape(1, 256),
            s5_glu_w[l].astype(BF16), s5_glu_b[l].reshape(1, 256).astype(F32),
            w_out[l].astype(BF16), norm2_w[l].reshape(1, d).astype(F32), rw, rb, lc, tm, need_ctx)

        rpb_rows = x_mid.shape[1]
        route2 = route.reshape(bsz * rpb_rows, 128)
        counts = cnt[0, MOE_GROUPS:MOE_GROUPS + MOE_EXPERTS].astype(jnp.int32)
        slots3, te, nact, n_slots = _dispatch_plan(route2, counts, tm, MOE_TILE)
        xs = _dispatch(h2, slots3, n_slots, tm)
        ys = _moe(xs, moe_w_gate, moe_w_up, moe_w_down, l, te, nact, MOE_TILE)
        res = _combine(x_mid.reshape(bsz * rpb_rows, d), route2, ys, slots3, msel,
                       final_norm_w.reshape(1, d).astype(F32), rpb_rows, lc if need_ctx else 0, tm, last)
        res = res.reshape(bsz, rpb_rows, d)
        if last:
            out = res
        else:
            x_all = res
    return out.astype(x.dtype)
```

```python
import functools
import math

import numpy as np
import jax
import jax.numpy as jnp
from jax import lax
from jax.experimental import pallas as pl
from jax.experimental.pallas import tpu as pltpu

F32 = jnp.float32
BF16 = jnp.bfloat16

GRID_W = 64
HEAD_DIM = 64
GROUP_WIDTH = 256
SWA_WINDOW = 128
ROPE_BASE = 10000.0
NA_ROWS = 8
NA_COLS = 16
HGRN_CHUNK = 64
HGRN_SUB = 16
S5_GROUP_CH = 16
S5_GROUPS = 16
S5_STATE = 64
S5_CHUNK = GROUP_WIDTH // S5_GROUP_CH
MOE_GROUPS = 4
MOE_PER_GROUP = 8
MOE_EXPERTS = 32
MOE_HIDDEN = 512
NORM_EPS = 1e-6
NEG_INF = -1e30
EXP_CLAMP = 80.0

V7X_VMEM_LIMIT_BYTES = 56 * 1024 * 1024
TOKEN_TILE = 256
MOE_TILE = 256


def _cparams(sem):
    return pltpu.CompilerParams(dimension_semantics=sem, vmem_limit_bytes=V7X_VMEM_LIMIT_BYTES)


def _dot(a, b):
    return jnp.dot(a, b, preferred_element_type=F32)


def _dot_nt(a, b):
    return lax.dot_general(a, b, (((1,), (1,)), ((), ())), preferred_element_type=F32)


def _dot_tn(a, b):
    return lax.dot_general(a, b, (((0,), (0,)), ((), ())), preferred_element_type=F32)


def _split(a):
    hi = a.astype(BF16)
    lo = (a - hi.astype(F32)).astype(BF16)
    return hi, lo


def _dot_f32(a, b):
    ah, al = _split(a)
    bh, bl = _split(b)
    return _dot(ah, bh) + _dot(al, bh) + _dot(ah, bl)


def _sigmoid(x):
    return 1.0 / (1.0 + jnp.exp(-x))


def _silu(x):
    return x * _sigmoid(x)


def _mod_kernel(c_ref, w_ref, b_ref, o_ref):
    c = c_ref[...]
    o_ref[0] = _dot_f32(_silu(c), w_ref[0]) + b_ref[0]


def _mods(c_all, mod_w, mod_b):
    depth, d, d6 = mod_w.shape
    r = c_all.shape[0]
    bn = 1024
    return pl.pallas_call(
        _mod_kernel,
        grid=(depth, d6 // bn),
        in_specs=[pl.BlockSpec((r, d), lambda l, j: (0, 0)),
                  pl.BlockSpec((1, d, bn), lambda l, j: (l, 0, j)),
                  pl.BlockSpec((1, 1, bn), lambda l, j: (l, 0, j))],
        out_specs=pl.BlockSpec((1, r, bn), lambda l, j: (l, 0, j)),
        out_shape=jax.ShapeDtypeStruct((depth, r, d6), F32),
        compiler_params=_cparams(("arbitrary", "arbitrary")),
        name="mods",
    )(c_all, mod_w, mod_b.reshape(depth, 1, d6))


def _proj_kernel(x_ref, ms_ref, nw_ref, w_ref, cs_ref, sn_ref, qa_ref, qb_ref, hg_ref, u_ref, ub_ref):
    x = x_ref[0]
    var = jnp.mean(x * x, axis=-1, keepdims=True)
    y = x * lax.rsqrt(var + NORM_EPS) * nw_ref[...]
    shift = ms_ref[0, 0, 0:1, :]
    scale = ms_ref[0, 0, 1:2, :]
    h = (y * (1.0 + scale) + shift).astype(BF16)
    r = _dot(h, w_ref[:, 0:384])
    tm = r.shape[0]
    first_half = (lax.broadcasted_iota(jnp.int32, (tm, 128), 1) % 32) < 16
    for t in range(3):
        xt = r[:, 128 * t:128 * (t + 1)]
        partner = jnp.where(first_half, pltpu.roll(xt, 128 - 16, 1), pltpu.roll(xt, 16, 1))
        roped = xt * cs_ref[:, 128 * t:128 * (t + 1)] + partner * sn_ref[:, 128 * t:128 * (t + 1)]
        qa_ref[0, :, 128 * t:128 * (t + 1)] = roped.astype(BF16)
        if t == 2:
            qa_ref[0, :, 384:512] = pltpu.roll(roped, HEAD_DIM, 1).astype(BF16)
    t = _dot(h, w_ref[:, 384:1280])
    qa_ref[0, :, 512:640] = t[:, 0:128].astype(BF16)
    qa_ref[0, :, 640:768] = pltpu.roll(t[:, 0:128], HEAD_DIM, 1).astype(BF16)
    qb_ref[0, :, 0:256] = (t[:, 128:384] * (HEAD_DIM ** -0.5)).astype(BF16)
    qb_ref[0, :, 256:768] = t[:, 384:896].astype(BF16)
    hg_ref[0] = _dot(h, w_ref[:, 1280:2560])
    u = _dot(h, w_ref[:, 2560:2816])
    u_ref[0] = u
    ub_ref[0] = u.astype(BF16)


def _proj(x_all, msel, nw, w_ext, cs, sn, lc, tm):
    b, s, d = x_all.shape
    nt = s // tm
    seg = lambda j: jnp.where(j * tm >= lc, 1, 0)
    return pl.pallas_call(
        _proj_kernel,
        grid=(b, nt),
        in_specs=[pl.BlockSpec((1, tm, d), lambda i, j: (i, j, 0)),
                  pl.BlockSpec((1, 1, 6, d), lambda i, j: (i, seg(j), 0, 0)),
                  pl.BlockSpec((1, d), lambda i, j: (0, 0)),
                  pl.BlockSpec((d, 2816), lambda i, j: (0, 0)),
                  pl.BlockSpec((tm, 384), lambda i, j: (j, 0)),
                  pl.BlockSpec((tm, 384), lambda i, j: (j, 0))],
        out_specs=[pl.BlockSpec((1, tm, 768), lambda i, j: (i, j, 0)),
                   pl.BlockSpec((1, tm, 768), lambda i, j: (i, j, 0)),
                   pl.BlockSpec((1, tm, 1280), lambda i, j: (i, j, 0)),
                   pl.BlockSpec((1, tm, 256), lambda i, j: (i, j, 0)),
                   pl.BlockSpec((1, tm, 256), lambda i, j: (i, j, 0))],
        out_shape=[jax.ShapeDtypeStruct((b, s, 768), BF16),
                   jax.ShapeDtypeStruct((b, s, 768), BF16),
                   jax.ShapeDtypeStruct((b, s, 1280), F32),
                   jax.ShapeDtypeStruct((b, s, 256), F32),
                   jax.ShapeDtypeStruct((b, s, 256), BF16)],
        compiler_params=_cparams(("arbitrary", "arbitrary")),
        name="proj",
    )(x_all, msel, nw, w_ext, cs, sn)


def _attend(q, parts, sink):
    scores = []
    for k, _, mask in parts:
        s = _dot_nt(q, k)
        if mask is not None:
            s = s + mask
        scores.append(s)
    m = functools.reduce(jnp.maximum, [jnp.max(s, axis=-1, keepdims=True) for s in scores])
    if sink is not None:
        m = jnp.maximum(m, sink)
    den = jnp.zeros_like(m)
    out = None
    for s, (_, v, _) in zip(scores, parts):
        p = jnp.exp(s - m)
        den = den + jnp.sum(p, axis=-1, keepdims=True)
        o = _dot(p.astype(BF16), v)
        out = o if out is None else out + o
    if sink is not None:
        den = den + jnp.exp(sink - m)
    return out / den


def _swa_kernel(sink_ref, qa_ref, o_ref, *, lc, n, tq, nct):
    j = pl.program_id(1)
    win = tq + 2 * SWA_WINDOW
    lo = lax.broadcasted_iota(jnp.int32, (tq, 128), 1) < HEAD_DIM
    row_lo = lax.broadcasted_iota(jnp.int32, (2 * tq, 1), 0) < tq
    sink_a = jnp.where(row_lo, sink_ref[0], sink_ref[3])
    sink_b = jnp.where(row_lo, sink_ref[1], sink_ref[2])
    ctx = [qa_ref[0, 0:lc, 128 * t:128 * (t + 1)] for t in range(2, 6)]

    def heads(q, lat):
        zero = jnp.zeros_like(q[:, 0:128])
        q01, q23 = q[:, 0:128], q[:, 128:256]
        qa = jnp.concatenate([jnp.where(lo, q01, zero), jnp.where(lo, zero, q23)], axis=0)
        qb = jnp.concatenate([jnp.where(lo, zero, q01), jnp.where(lo, q23, zero)], axis=0)
        pa, pb = [], []
        if lat is not None:
            kw, kx, vw, vx, mask = lat
            mask2 = jnp.concatenate([mask, mask], axis=0)
            pa.append((kw, vw, mask2))
            pb.append((kx, vx, mask2))
        pa.append((ctx[0], ctx[2], None))
        pb.append((ctx[1], ctx[3], None))
        oa = _attend(qa, pa, sink_a)
        ob = _attend(qb, pb, sink_b)
        out01 = jnp.where(lo, oa[0:tq], ob[0:tq])
        out23 = jnp.where(lo, ob[tq:2 * tq], oa[tq:2 * tq])
        return jnp.concatenate([out01, out23], axis=-1).astype(BF16)

    if nct > 0:
        @pl.when(j < nct)
        def _():
            r0 = pl.multiple_of(j * tq, tq)
            o_ref[0] = heads(qa_ref[0, pl.ds(r0, tq), 0:256], None)

    @pl.when(j >= nct)
    def _():
        i = j - nct
        q0 = pl.multiple_of(lc + i * tq, 16)
        ks = jnp.clip(i * tq - SWA_WINDOW, 0, n - win)
        k0 = pl.multiple_of(lc + ks, 16)
        q = qa_ref[0, pl.ds(q0, tq), 0:256]
        lat = [qa_ref[0, pl.ds(k0, win), 128 * t:128 * (t + 1)] for t in range(2, 6)]
        qpos = i * tq + lax.broadcasted_iota(jnp.int32, (tq, win), 0)
        kpos = ks + lax.broadcasted_iota(jnp.int32, (tq, win), 1)
        mask = jnp.where(jnp.abs(qpos - kpos) <= SWA_WINDOW, 0.0, NEG_INF).astype(F32)
        o_ref[0] = heads(q, lat + [mask])


def _swa(qa, sink, lc, need_ctx):
    b, s, _ = qa.shape
    n = s - lc
    tq = min(128, lc)
    nct = lc // tq if need_ctx else 0
    return pl.pallas_call(
        functools.partial(_swa_kernel, lc=lc, n=n, tq=tq, nct=nct),
        grid=(b, nct + n // tq),
        in_specs=[pl.BlockSpec(memory_space=pltpu.SMEM),
                  pl.BlockSpec((1, s, 768), lambda i, j: (i, 0, 0))],
        out_specs=pl.BlockSpec((1, tq, 256), lambda i, j: (i, j, 0)),
        out_shape=jax.ShapeDtypeStruct((b, nct * tq + n, 256), BF16),
        compiler_params=_cparams(("arbitrary", "arbitrary")),
        name="swa",
    )(sink, qa)


def _na_kernel(qb_ref, *rest, lc, rows, nct, per_step):
    bias_refs, o_ref = rest[:per_step], rest[per_step]
    j = pl.program_id(1)
    nk = NA_ROWS * GRID_W
    kc = qb_ref[0, 0:lc, 256:512]
    vc = qb_ref[0, 0:lc, 512:768]

    m = GRID_W
    rb = lax.broadcasted_iota(jnp.int32, (4 * m, 256), 0) // m
    lb = lax.broadcasted_iota(jnp.int32, (4 * m, 256), 1) // HEAD_DIM
    own = rb == lb

    def heads(q, lat):
        qbd = jnp.where(own, jnp.concatenate([q] * 4, axis=0), jnp.zeros((4 * m, 256), BF16))
        parts = []
        if lat is not None:
            kw, vw, bias = lat
            parts.append((kw, vw, bias))
        parts.append((kc, vc, None))
        o4 = jnp.where(own, _attend(qbd, parts, None), 0.0)
        return (o4[0:m] + o4[m:2 * m] + o4[2 * m:3 * m] + o4[3 * m:4 * m]).astype(BF16)

    if nct > 0:
        @pl.when(j < nct)
        def _():
            for t in range(per_step):
                r0 = pl.multiple_of((j * per_step + t) * m, m)
                o_ref[0, m * t:m * (t + 1), :] = heads(qb_ref[0, pl.ds(r0, m), 0:256], None)

    @pl.when(j >= nct)
    def _():
        for t in range(per_step):
            r = (j - nct) * per_step + t
            rs = jnp.clip(r - NA_ROWS // 2, 0, rows - NA_ROWS)
            q0 = pl.multiple_of(lc + r * m, m)
            k0 = pl.multiple_of(lc + rs * m, m)
            q = qb_ref[0, pl.ds(q0, m), 0:256]
            kw = qb_ref[0, pl.ds(k0, nk), 256:512]
            vw = qb_ref[0, pl.ds(k0, nk), 512:768]
            o_ref[0, m * t:m * (t + 1), :] = heads(q, (kw, vw, bias_refs[t][0]))


def _na(qb, bias, lc, need_ctx):
    b, s, _ = qb.shape
    n = s - lc
    rows = n // GRID_W
    ctx_tiles = lc // GRID_W if need_ctx else 0
    per_step = 2 if (ctx_tiles % 2 == 0 and rows % 2 == 0) else 1
    nct = ctx_tiles // per_step

    def cls(t):
        def index(i, j):
            r = jnp.maximum((j - nct) * per_step + t, 0)
            return (r - jnp.clip(r - NA_ROWS // 2, 0, rows - NA_ROWS), 0, 0)
        return index

    return pl.pallas_call(
        functools.partial(_na_kernel, lc=lc, rows=rows, nct=nct, per_step=per_step),
        grid=(b, nct + rows // per_step),
        in_specs=[pl.BlockSpec((1, s, 768), lambda i, j: (i, 0, 0))]
                 + [pl.BlockSpec((1, 4 * GRID_W, NA_ROWS * GRID_W), cls(t)) for t in range(per_step)],
        out_specs=pl.BlockSpec((1, per_step * GRID_W, 256), lambda i, j: (i, j, 0)),
        out_shape=jax.ShapeDtypeStruct((b, ctx_tiles * GRID_W + n, 256), BF16),
        compiler_params=_cparams(("arbitrary", "arbitrary")),
        name="na",
    )(qb, *([bias] * per_step))


def _na_bias(rpb):
    c = np.arange(GRID_W)[:, None]
    kc = np.arange(GRID_W)[None, :]
    ci = np.clip(kc - c + NA_COLS - 1, 0, 2 * NA_COLS - 2)
    ws = np.clip(c - NA_COLS // 2, 0, GRID_W - NA_COLS)
    valid = (kc >= ws) & (kc < ws + NA_COLS)
    sel = (ci[None] == np.arange(2 * NA_COLS - 1)[:, None, None]).astype(np.float32)
    t = jnp.einsum('hrd,dck->hrck', rpb.astype(F32), sel, precision=lax.Precision.HIGHEST)
    t = jnp.where(valid[None, None], t, NEG_INF)
    per_cls = [t[:, NA_ROWS - 1 - d:2 * NA_ROWS - 1 - d] for d in range(NA_ROWS)]
    t = jnp.transpose(jnp.stack(per_cls, axis=0), (0, 1, 3, 2, 4))
    return t.reshape(NA_ROWS, 4 * GRID_W, NA_ROWS * GRID_W)


def _hgrn_dir(blk, lb, st_ref, rev):
    c = HGRN_CHUNK
    sub = HGRN_SUB
    nsub = c // sub
    q = _silu(blk[:, 0:256])
    fraw = blk[:, 512:768] if rev else blk[:, 256:512]
    v = blk[:, 768:1024]
    f = lb + (1.0 - lb) * _sigmoid(fraw)
    kk = 1.0 - f
    logf = jnp.log(f)
    ti = lax.broadcasted_iota(jnp.int32, (c, c), 0)
    si = lax.broadcasted_iota(jnp.int32, (c, c), 1)
    tri = jnp.where((si >= ti) if rev else (si <= ti), 1.0, 0.0).astype(BF16)
    bcum = _dot_exact_rhs_left(tri, logf)
    btot = bcum[0:1] if rev else bcum[c - 1:c]
    vb = v.astype(BF16)
    st = st_ref[...]
    o_inter = _dot_nt((q * jnp.exp(bcum)).astype(BF16), st.astype(BF16))

    rh = lax.broadcasted_iota(jnp.int32, (c, 256), 0) // sub
    lh = lax.broadcasted_iota(jnp.int32, (c, 256), 1) // HEAD_DIM
    bd = jnp.where(rh == lh, 1.0, 0.0).astype(F32)
    outs = []
    for i in range(nsub):
        r0, r1 = i * sub, (i + 1) * sub
        if rev:
            rho = bcum[r1:r1 + 1] if i < nsub - 1 else jnp.zeros((1, 256), F32)
            k0, k1 = r0, c
        else:
            rho = bcum[r0 - 1:r0] if i > 0 else jnp.zeros((1, 256), F32)
            k0, k1 = 0, r1
        qh = q[r0:r1] * jnp.exp(bcum[r0:r1] - rho)
        kh = kk[k0:k1] * jnp.exp(jnp.minimum(rho - bcum[k0:k1], EXP_CLAMP))
        qbd = (jnp.concatenate([qh] * 4, axis=0) * bd).astype(BF16)
        a = _dot_nt(qbd, kh.astype(BF16))
        nk = k1 - k0
        tq = r0 + lax.broadcasted_iota(jnp.int32, (c, nk), 0) % sub
        sk = k0 + lax.broadcasted_iota(jnp.int32, (c, nk), 1)
        a = jnp.where((sk >= tq) if rev else (sk <= tq), a, 0.0)
        o4 = _dot(a.astype(BF16), vb[k0:k1]) * bd
        outs.append(o4[0:sub] + o4[sub:2 * sub] + o4[2 * sub:3 * sub] + o4[3 * sub:4 * sub])
    o = o_inter + jnp.concatenate(outs, axis=0)

    kend = (kk * jnp.exp(btot - bcum)).astype(BF16)
    r2 = lax.broadcasted_iota(jnp.int32, (256, 256), 0) // HEAD_DIM
    l2 = lax.broadcasted_iota(jnp.int32, (256, 256), 1) // HEAD_DIM
    upd = jnp.where(r2 == l2, _dot_tn(vb, kend), 0.0)
    st_ref[...] = st * jnp.exp(btot) + upd
    return o


def _dot_exact_rhs_left(tri_bf16, a):
    a0, a1 = _split(a)
    return _dot(tri_bf16, a0) + _dot(tri_bf16, a1)


HGRN_BATCH = 4


def _hgrn_kernel(hf_ref, hb_ref, lb_ref, of_ref, ob_ref, stf_ref, stb_ref):
    @pl.when(pl.program_id(1) == 0)
    def _():
        stf_ref[...] = jnp.zeros_like(stf_ref)
        stb_ref[...] = jnp.zeros_like(stb_ref)

    for bi in range(HGRN_BATCH):
        of_ref[bi] = _hgrn_dir(hf_ref[bi], lb_ref[0:1, :], stf_ref.at[bi], False)
        ob_ref[bi] = _hgrn_dir(hb_ref[bi], lb_ref[1:2, :], stb_ref.at[bi], True)


def _hgrn(hg, lb2, lc):
    b, s, _ = hg.shape
    c = HGRN_CHUNK
    nc = s // c
    ncc = lc // c

    def bwd(j):
        return jnp.where(j < ncc, ncc - 1 - j, ncc + nc - 1 - j)

    hb = HGRN_BATCH
    return pl.pallas_call(
        _hgrn_kernel,
        grid=(b // hb, nc),
        in_specs=[pl.BlockSpec((hb, c, 1280), lambda i, j: (i, j, 0)),
                  pl.BlockSpec((hb, c, 1280), lambda i, j: (i, bwd(j), 0)),
                  pl.BlockSpec((2, 256), lambda i, j: (0, 0))],
        out_specs=[pl.BlockSpec((hb, c, 256), lambda i, j: (i, j, 0)),
                   pl.BlockSpec((hb, c, 256), lambda i, j: (i, bwd(j), 0))],
        out_shape=[jax.ShapeDtypeStruct((b, s, 256), F32),
                   jax.ShapeDtypeStruct((b, s, 256), F32)],
        scratch_shapes=[pltpu.VMEM((hb, 256, 256), F32), pltpu.VMEM((hb, 256, 256), F32)],
        compiler_params=_cparams(("arbitrary", "arbitrary")),
        name="hgrn",
    )(hg, hg, lb2)


def _s5_kernel(u_ref, k_ref, bm_ref, cm_ref, at_ref, y_ref,
               hin_ref, hst_ref, *, bsz, ncc, nc):
    ub = u_ref[0]
    for m in range(4):
        hin_ref[m] = _dot(ub, bm_ref[0, m])

    def run(direction, order_fn, count, h0):
        a_re = at_ref[0, 2 * direction:2 * direction + 1, :]
        a_im = at_ref[0, 2 * direction + 1:2 * direction + 2, :]

        def body(t, carry):
            h_re, h_im = carry
            r0 = pl.multiple_of(order_fn(t) * bsz, 8)
            hst_ref[2 * direction, pl.ds(r0, bsz), :] = h_re
            hst_ref[2 * direction + 1, pl.ds(r0, bsz), :] = h_im
            n_re = a_re * h_re - a_im * h_im + hin_ref[2 * direction, pl.ds(r0, bsz), :]
            n_im = a_re * h_im + a_im * h_re + hin_ref[2 * direction + 1, pl.ds(r0, bsz), :]
            return n_re, n_im

        return lax.fori_loop(0, count, body, h0)

    z = jnp.zeros((bsz, S5_STATE), F32)
    run(0, lambda t: t, nc, (z, z))
    hb = run(1, lambda t: ncc - 1 - t, ncc, (z, z))
    run(1, lambda t: nc - 1 - t, nc - ncc, hb)

    y = _dot(ub, k_ref[0])
    for m in range(4):
        y = y + _dot(hst_ref[m].astype(BF16), cm_ref[0, m])
    y_ref[0] = y.astype(BF16)


def _s5(ug, kmat, bmat, cmat, at, bsz, ncc):
    g, rows, _ = ug.shape
    nc = rows // bsz
    return pl.pallas_call(
        functools.partial(_s5_kernel, bsz=bsz, ncc=ncc, nc=nc),
        grid=(g,),
        in_specs=[pl.BlockSpec((1, rows, 256), lambda i: (i, 0, 0)),
                  pl.BlockSpec((1, 256, 256), lambda i: (i, 0, 0)),
                  pl.BlockSpec((1, 4, 256, S5_STATE), lambda i: (i, 0, 0, 0)),
                  pl.BlockSpec((1, 4, S5_STATE, 256), lambda i: (i, 0, 0, 0)),
                  pl.BlockSpec((1, 4, S5_STATE), lambda i: (i, 0, 0))],
        out_specs=pl.BlockSpec((1, rows, 256), lambda i: (i, 0, 0)),
        out_shape=jax.ShapeDtypeStruct((g, rows, 256), BF16),
        scratch_shapes=[pltpu.VMEM((4, rows, S5_STATE), F32), pltpu.VMEM((4, rows, S5_STATE), F32)],
        compiler_params=_cparams(("arbitrary",)),
        name="s5",
    )(ug, kmat, bmat, cmat, at)


def _s5_matrices(lam_re, lam_im, log_dt, b_re, b_im, c_re, c_im):
    t = S5_CHUNK
    hp = lax.Precision.HIGHEST
    lam_re, lam_im, log_dt = lam_re.astype(F32), lam_im.astype(F32), log_dt.astype(F32)
    b_re, b_im, c_re, c_im = b_re.astype(F32), b_im.astype(F32), c_re.astype(F32), c_im.astype(F32)
    dt = jnp.exp(log_dt)[..., None]
    jj = jnp.arange(t + 1, dtype=F32)[:, None, None, None]
    mag = jnp.exp(lam_re * dt * jj)
    ang = lam_im * dt * jj
    aj_re, aj_im = mag * jnp.cos(ang), mag * jnp.sin(ang)
    a_re, a_im = aj_re[1], aj_im[1]
    den = lam_re * lam_re + lam_im * lam_im
    k_re = ((a_re - 1.0) * lam_re + a_im * lam_im) / den
    k_im = (a_im * lam_re - (a_re - 1.0) * lam_im) / den
    bb_re = k_re[..., None] * b_re - k_im[..., None] * b_im
    bb_im = k_re[..., None] * b_im + k_im[..., None] * b_re
    ab_re = aj_re[..., None] * bb_re - aj_im[..., None] * bb_im
    ab_im = aj_re[..., None] * bb_im + aj_im[..., None] * bb_re
    kj = (jnp.einsum('dghp,jdgpi->jdghi', c_re, ab_re, precision=hp)
          - jnp.einsum('dghp,jdgpi->jdghi', c_im, ab_im, precision=hp))
    s_i = np.arange(t)[:, None]
    t_i = np.arange(t)[None, :]
    kf = jnp.where((t_i >= s_i)[:, :, None, None, None], kj[np.clip(t_i - s_i, 0, t), 0], 0.0)
    kb = jnp.where((s_i >= t_i)[:, :, None, None, None], kj[np.clip(s_i - t_i, 0, t), 1], 0.0)
    kmat = jnp.transpose(kf + kb, (2, 0, 4, 1, 3)).reshape(S5_GROUPS, 256, 256)

    def state_in(ab, idx, direction):
        return jnp.transpose(ab[idx, direction], (1, 0, 3, 2)).reshape(S5_GROUPS, 256, S5_STATE)

    fw_idx = np.arange(t - 1, -1, -1)
    bw_idx = np.arange(t)
    bmat = jnp.stack([state_in(ab_re, fw_idx, 0), state_in(ab_im, fw_idx, 0),
                      state_in(ab_re, bw_idx, 1), state_in(ab_im, bw_idx, 1)], axis=1)

    def state_out(idx, direction):
        ar = aj_re[idx, direction][:, :, None, :]
        ai = aj_im[idx, direction][:, :, None, :]
        cr, ci = c_re[direction][None], c_im[direction][None]
        re = cr * ar - ci * ai
        im = -(cr * ai + ci * ar)
        f = lambda m: jnp.transpose(m, (1, 3, 0, 2)).reshape(S5_GROUPS, S5_STATE, 256)
        return f(re), f(im)

    cf_re, cf_im = state_out(np.arange(1, t + 1), 0)
    cb_re, cb_im = state_out(np.arange(t, 0, -1), 1)
    cmat = jnp.stack([cf_re, cf_im, cb_re, cb_im], axis=1)
    at = jnp.stack([aj_re[t, 0], aj_im[t, 0], aj_re[t, 1], aj_im[t, 1]], axis=1)
    return kmat.astype(BF16), bmat.astype(BF16), cmat.astype(BF16), at


def _gelu_tanh(x):
    return 0.5 * x * (1.0 + jnp.tanh(math.sqrt(2.0 / math.pi) * (x + 0.044715 * (x * x * x))))


def _store_token_tiles(ref, val):
    m = val.shape[0]
    for k in range(8):
        ref[pl.ds(k, m, stride=8), :] = val[:, 128 * k:128 * (k + 1)]


def _load_token_tiles(ref, m):
    return jnp.concatenate([ref[pl.ds(k, m, stride=8), :] for k in range(8)], axis=-1)


def _outproj_kernel(a_ref, b_ref, of_ref, ob_ref, g_ref, y5_ref, u_ref, d_ref, x_ref, ms_ref, hnw_ref,
                    gw_ref, gb_ref, wo_ref, n2_ref, rw_ref, rb_ref,
                    xo_ref, h2_ref, rt_ref, cnt_ref, run_ref):
    @pl.when((pl.program_id(0) == 0) & (pl.program_id(1) == 0))
    def _():
        run_ref[...] = jnp.zeros_like(run_ref)

    o = of_ref[0] + ob_ref[0]
    r2 = lax.broadcasted_iota(jnp.int32, (256, 256), 0) // HEAD_DIM
    l2 = lax.broadcasted_iota(jnp.int32, (256, 256), 1) // HEAD_DIM
    hm = jnp.where(r2 == l2, 1.0, 0.0).astype(BF16)
    sq = o * o
    sh, sl = _split(sq)
    ms = (_dot(sh, hm) + _dot(sl, hm)) * (1.0 / HEAD_DIM)
    cm = o * lax.rsqrt(ms + NORM_EPS) * hnw_ref[...] * _silu(g_ref[0])
    z = _gelu_tanh(y5_ref[0].astype(F32) + d_ref[...] * u_ref[0])
    dm = z * _sigmoid(_dot(z.astype(BF16), gw_ref[...]) + gb_ref[...])
    mix = (_dot(a_ref[0], wo_ref[0:256, :]) + _dot(b_ref[0], wo_ref[256:512, :])
           + _dot(cm.astype(BF16), wo_ref[512:768, :]) + _dot(dm.astype(BF16), wo_ref[768:1024, :]))
    x = x_ref[0] + ms_ref[0, 0, 2:3, :] * mix
    xo_ref[0] = x
    var = jnp.mean(x * x, axis=-1, keepdims=True)
    h2 = x * lax.rsqrt(var + NORM_EPS) * n2_ref[...]
    h2 = h2 * (1.0 + ms_ref[0, 0, 4:5, :]) + ms_ref[0, 0, 3:4, :]
    _store_token_tiles(h2_ref, h2)
    lg = _dot_f32(h2, rw_ref[...]) + rb_ref[...]
    tm = lg.shape[0]
    lane = lax.broadcasted_iota(jnp.int32, (tm, 128), 1)
    big = jnp.int32(1 << 20)
    ninf = jnp.float32(-jnp.inf)
    gmask = lane < MOE_GROUPS
    gl = jnp.where(gmask, lg, ninf)
    gmax = jnp.max(gl, axis=-1, keepdims=True)
    gidx = jnp.min(jnp.where(gl == gmax, lane, big), axis=-1, keepdims=True)
    gsum = jnp.sum(jnp.where(gmask, jnp.exp(gl - gmax), 0.0), axis=-1, keepdims=True)
    gwt = 1.0 / gsum
    e = lane - MOE_GROUPS
    emask = (e >= 0) & (e < MOE_EXPERTS) & (lax.shift_right_arithmetic(e, 3) == gidx)
    el = jnp.where(emask, lg, ninf)
    v1 = jnp.max(el, axis=-1, keepdims=True)
    i1 = jnp.min(jnp.where(el == v1, lane, big), axis=-1, keepdims=True)
    el2 = jnp.where(lane == i1, ninf, el)
    v2 = jnp.max(el2, axis=-1, keepdims=True)
    i2 = jnp.min(jnp.where(el2 == v2, lane, big), axis=-1, keepdims=True)
    t2 = jnp.exp(v2 - v1)
    w1 = gwt / (1.0 + t2)
    w2 = w1 * t2
    pick1 = lane == i1
    pick2 = lane == i2
    oh = jnp.where(pick1 | pick2, 1.0, 0.0).astype(F32)
    ti = lax.broadcasted_iota(jnp.int32, (tm, tm), 0)
    si = lax.broadcasted_iota(jnp.int32, (tm, tm), 1)
    before = _dot(jnp.where(si < ti, 1.0, 0.0).astype(BF16), oh.astype(BF16)) + run_ref[0:1, :]
    r1 = jnp.sum(jnp.where(pick1, before, 0.0), axis=-1, keepdims=True)
    r2 = jnp.sum(jnp.where(pick2, before, 0.0), axis=-1, keepdims=True)
    run = run_ref[...] + jnp.sum(oh, axis=0, keepdims=True)
    run_ref[...] = run
    cnt_ref[...] = run
    vals = ((i1 - MOE_GROUPS).astype(F32), (i2 - MOE_GROUPS).astype(F32), w1, w2, r1, r2)
    rt = jnp.zeros((tm, 128), F32)
    for k, val in enumerate(vals):
        rt = jnp.where(lane == k, val, rt)
    rt_ref[0] = rt


def _outproj(a_out, b_out, o_f, o_b, hg, y5, u, dvec, x_all, msel, hnw, glu_w, glu_b, w_out, n2w, rw, rb,
             lc, tm, need_ctx):
    b, s, d = x_all.shape
    off = 0 if need_ctx else lc // tm
    nt = s // tm - off
    rows = nt * tm
    seg = lambda j: jnp.where((j + off) * tm >= lc, 1, 0)
    tok = lambda w: pl.BlockSpec((1, tm, w), lambda i, j: (i, j + off, 0))
    loc = lambda w: pl.BlockSpec((1, tm, w), lambda i, j: (i, j, 0))
    full = lambda r, c: pl.BlockSpec((r, c), lambda i, j: (0, 0))
    return pl.pallas_call(
        _outproj_kernel,
        grid=(b, nt),
        in_specs=[loc(256), loc(256), tok(256), tok(256),
                  pl.BlockSpec((1, tm, 256), lambda i, j: (i, j + off, 4)),
                  tok(256), tok(256), full(1, 256), tok(d),
                  pl.BlockSpec((1, 1, 6, d), lambda i, j: (i, seg(j), 0, 0)),
                  full(1, 256), full(256, 256), full(1, 256), full(d, d), full(1, d),
                  full(d, 128), full(1, 128)],
        out_specs=[loc(d), pl.BlockSpec((tm * 8, 128), lambda i, j: (i * nt + j, 0)), loc(128),
                   full(8, 128)],
        out_shape=[jax.ShapeDtypeStruct((b, rows, d), F32),
                   jax.ShapeDtypeStruct((b * rows * 8, 128), F32),
                   jax.ShapeDtypeStruct((b, rows, 128), F32),
                   jax.ShapeDtypeStruct((8, 128), F32)],
        scratch_shapes=[pltpu.VMEM((8, 128), F32)],
        compiler_params=_cparams(("arbitrary", "arbitrary")),
        name="outproj",
    )(a_out, b_out, o_f, o_b, hg, y5, u, dvec, x_all, msel, hnw, glu_w, glu_b, w_out, n2w, rw, rb)


DMA_UNROLL = 8


def _dispatch_kernel(slot_ref, h_ref, xs_in, xs_out, sem, *, tm):
    del xs_in

    def tok_copy(r, k, dst):
        src = pl.multiple_of(r * 8, 8)
        return pltpu.make_async_copy(h_ref.at[pl.ds(src, 8)], xs_out.at[pl.ds(pl.multiple_of(dst, 8), 8)],
                                     sem.at[k])

    def issue(r, c):
        tok_copy(r, 0, slot_ref[0, 0, r]).start()
        tok_copy(r, 1, slot_ref[0, 1, r]).start()
        return c

    lax.fori_loop(0, tm, issue, 0, unroll=DMA_UNROLL)
    for k in range(2):
        pltpu.make_async_copy(h_ref, xs_out.at[pl.ds(0, tm * 8)], sem.at[k]).wait()


def _dispatch(h2_tiles, slots3, n_slots, tm):
    rows8 = h2_tiles.shape[0]
    xs0 = jnp.zeros((n_slots * 8, 128), F32)
    return pl.pallas_call(
        functools.partial(_dispatch_kernel, tm=tm),
        grid=(rows8 // (tm * 8),),
        in_specs=[pl.BlockSpec((1, 2, tm), lambda i: (i, 0, 0), memory_space=pltpu.SMEM),
                  pl.BlockSpec((tm * 8, 128), lambda i: (i, 0)),
                  pl.BlockSpec(memory_space=pl.ANY)],
        out_specs=pl.BlockSpec(memory_space=pl.ANY),
        out_shape=jax.ShapeDtypeStruct((n_slots * 8, 128), F32),
        scratch_shapes=[pltpu.SemaphoreType.DMA((2,))],
        input_output_aliases={2: 0},
        compiler_params=_cparams(("arbitrary",)),
        name="dispatch",
    )(slots3, h2_tiles, xs0)


def _moe_kernel(te_ref, nact_ref, x_ref, wg_ref, wu_ref, wd_ref, y_ref, wgb, wub, wdb, *, tm):
    i = pl.program_id(0)

    @pl.when((i == 0) | (te_ref[i] != te_ref[jnp.maximum(i - 1, 0)]))
    def _():
        wgb[...] = wg_ref[0, 0].astype(BF16)
        wub[...] = wu_ref[0, 0].astype(BF16)
        wdb[...] = wd_ref[0, 0].astype(BF16)

    @pl.when(i < nact_ref[0])
    def _():
        gate = None
        up = None
        for p in range(4):
            xp = jnp.concatenate([x_ref[pl.ds(2 * p, tm, stride=8), :],
                                  x_ref[pl.ds(2 * p + 1, tm, stride=8), :]], axis=-1).astype(BF16)
            g = _dot(xp, wgb[256 * p:256 * (p + 1), :])
            u = _dot(xp, wub[256 * p:256 * (p + 1), :])
            gate = g if gate is None else gate + g
            up = u if up is None else up + u
        hid = (_silu(gate) * up).astype(BF16)
        _store_token_tiles(y_ref, _dot(hid, wdb[...]))

    @pl.when(i >= nact_ref[0])
    def _():
        y_ref[...] = jnp.zeros_like(y_ref)


def _moe(xs, wg, wu, wd, layer, tile_expert, nact, tm):
    n_slots = xs.shape[0] // 8
    _, _, d, hid = wg.shape
    nt = n_slots // tm
    return pl.pallas_call(
        functools.partial(_moe_kernel, tm=tm),
        grid_spec=pltpu.PrefetchScalarGridSpec(
            num_scalar_prefetch=2,
            grid=(nt,),
            in_specs=[pl.BlockSpec((tm * 8, 128), lambda i, te, na: (i, 0)),
                      pl.BlockSpec((1, 1, d, hid), lambda i, te, na: (layer, te[i], 0, 0)),
                      pl.BlockSpec((1, 1, d, hid), lambda i, te, na: (layer, te[i], 0, 0)),
                      pl.BlockSpec((1, 1, hid, d), lambda i, te, na: (layer, te[i], 0, 0))],
            out_specs=pl.BlockSpec((tm * 8, 128), lambda i, te, na: (i, 0)),
            scratch_shapes=[pltpu.VMEM((d, hid), BF16), pltpu.VMEM((d, hid), BF16),
                            pltpu.VMEM((hid, d), BF16)]),
        out_shape=jax.ShapeDtypeStruct((n_slots * 8, 128), F32),
        compiler_params=_cparams(("arbitrary",)),
        name="moe",
    )(tile_expert, nact, xs, wg, wu, wd)


def _dispatch_plan(route, counts, tok_tile, moe_tile):
    t = route.shape[0]
    eid = route[:, 0:2].astype(jnp.int32)
    rank = route[:, 4:6].astype(jnp.int32)
    padded = ((counts + moe_tile - 1) // moe_tile) * moe_tile
    pad_end = jnp.cumsum(padded)
    pad_start = pad_end - padded
    slots = (jnp.take(pad_start, eid, axis=0) + rank) * 8
    slots3 = jnp.transpose(slots.reshape(t // tok_tile, tok_tile, 2), (0, 2, 1))
    nt = (2 * t) // moe_tile + MOE_EXPERTS
    nact = (pad_end[-1] // moe_tile).astype(jnp.int32)
    tile_start = jnp.arange(nt, dtype=jnp.int32) * moe_tile
    te = jnp.sum((pad_end[None, :] <= tile_start[:, None]).astype(jnp.int32), axis=1)
    last_used = jnp.max(jnp.where(counts > 0, jnp.arange(MOE_EXPERTS, dtype=jnp.int32), 0))
    te = jnp.minimum(te, last_used)
    return slots3, te, nact.reshape(1), nt * moe_tile


def _combine_kernel(slot_ref, nslot_ref, x_ref, rt_ref, ms_ref, fw_ref, ys_hbm, o_ref, ybuf, sem,
                    *, final, tm):
    i = pl.program_id(0)
    n = pl.num_programs(0)
    cur = lax.rem(i, 2)

    def tok_copy(src, buf, k, r):
        dst = pl.multiple_of(r * 8, 8)
        return pltpu.make_async_copy(ys_hbm.at[pl.ds(pl.multiple_of(src, 8), 8)],
                                     ybuf.at[buf, k, pl.ds(dst, 8)], sem.at[buf])

    def issue(table, buf):
        def body(r, c):
            tok_copy(table[0, 0, r], buf, 0, r).start()
            tok_copy(table[0, 1, r], buf, 1, r).start()
            return c
        lax.fori_loop(0, tm, body, 0, unroll=DMA_UNROLL)

    @pl.when(i == 0)
    def _():
        issue(slot_ref, 0)

    @pl.when(i + 1 < n)
    def _():
        issue(nslot_ref, 1 - cur)

    for k in range(2):
        pltpu.make_async_copy(ys_hbm.at[pl.ds(0, tm * 8)], ybuf.at[cur, k], sem.at[cur]).wait()

    w0 = rt_ref[:, 2:3]
    w1 = rt_ref[:, 3:4]
    y0 = _load_token_tiles(ybuf.at[cur, 0], tm)
    y1 = _load_token_tiles(ybuf.at[cur, 1], tm)
    x = x_ref[...] + ms_ref[0, 0, 5:6, :] * (w0 * y0 + w1 * y1)
    if final:
        var = jnp.mean(x * x, axis=-1, keepdims=True)
        x = x * lax.rsqrt(var + NORM_EPS) * fw_ref[...]
    o_ref[...] = x


def _combine(x_mid, route, ys, slots3, msel, fw, rows_per_batch, ctx_rows, tm, final):
    rows, d = x_mid.shape
    nt = rows // tm
    ntb = rows_per_batch // tm
    seg = lambda i: jnp.where((i % ntb) * tm >= ctx_rows, 1, 0)
    smem_tile = lambda fn: pl.BlockSpec((1, 2, tm), fn, memory_space=pltpu.SMEM)
    return pl.pallas_call(
        functools.partial(_combine_kernel, final=final, tm=tm),
        grid=(nt,),
        in_specs=[smem_tile(lambda i: (i, 0, 0)),
                  smem_tile(lambda i: (jnp.minimum(i + 1, nt - 1), 0, 0)),
                  pl.BlockSpec((tm, d), lambda i: (i, 0)),
                  pl.BlockSpec((tm, 128), lambda i: (i, 0)),
                  pl.BlockSpec((1, 1, 6, d), lambda i: (i // ntb, seg(i), 0, 0)),
                  pl.BlockSpec((1, d), lambda i: (0, 0)),
                  pl.BlockSpec(memory_space=pl.ANY)],
        out_specs=pl.BlockSpec((tm, d), lambda i: (i, 0)),
        out_shape=jax.ShapeDtypeStruct((rows, d), F32),
        scratch_shapes=[pltpu.VMEM((2, 2, tm * 8, 128), F32), pltpu.SemaphoreType.DMA((2,))],
        compiler_params=_cparams(("arbitrary",)),
        name="combine",
    )(slots3, slots3, x_mid, route, msel, fw, ys)


def _rope_tables(n, lc):
    half = HEAD_DIM // 2
    inv = 1.0 / (ROPE_BASE ** (np.arange(0, half, 2, dtype=np.float64) / half))
    t = np.arange(n)
    ang_r = (t // GRID_W).astype(np.float64)[:, None] * inv[None, :]
    ang_c = (t % GRID_W).astype(np.float64)[:, None] * inv[None, :]
    cos_h = np.concatenate([np.cos(ang_r), np.cos(ang_r), np.cos(ang_c), np.cos(ang_c)], axis=-1)
    sin_h = np.concatenate([-np.sin(ang_r), np.sin(ang_r), -np.sin(ang_c), np.sin(ang_c)], axis=-1)
    cos_h = np.concatenate([np.ones((lc, HEAD_DIM)), cos_h], axis=0)
    sin_h = np.concatenate([np.zeros((lc, HEAD_DIM)), sin_h], axis=0)
    qs = HEAD_DIM ** -0.5
    cs = np.concatenate([np.tile(cos_h, (1, 4)) * qs, np.tile(cos_h, (1, 2))], axis=-1)
    sn = np.concatenate([np.tile(sin_h, (1, 4)) * qs, np.tile(sin_h, (1, 2))], axis=-1)
    return jnp.asarray(cs, F32), jnp.asarray(sn, F32)


def kernel(x, c, ctx, c_ctx, mod_w, mod_b, norm1_w, norm2_w, w_in, w_out, swa_sink, na_rpb, hgrn_lb,
           hgrn_norm_w, s5_lam_re, s5_lam_im, s5_log_dt, s5_b_re, s5_b_im, s5_c_re, s5_c_im, s5_d,
           s5_glu_w, s5_glu_b, moe_group_w, moe_group_b, moe_expert_w, moe_expert_b, moe_w_gate,
           moe_w_up, moe_w_down, final_norm_w):
    bsz, n, d = x.shape
    lc = ctx.shape[1]
    s = lc + n
    depth = mod_w.shape[0]
    tm = min(TOKEN_TILE, lc)
    assert d == 1024 and lc % tm == 0 and n % tm == 0 and lc % HGRN_CHUNK == 0 and bsz % 8 == 0
    assert n % GRID_W == 0 and n // GRID_W >= NA_ROWS and n >= min(128, lc) + 2 * SWA_WINDOW

    x_all = jnp.concatenate([ctx, x], axis=1).astype(F32)

    rpad = (-(bsz + 1)) % 8
    c_all = jnp.concatenate([c, c_ctx[None], jnp.zeros((rpad, d), c.dtype)], axis=0).astype(F32)
    mods = _mods(c_all, mod_w.astype(F32), mod_b.astype(F32))
    mods = mods.reshape(depth, -1, 6, d)

    lbp = jax.nn.softmax(hgrn_lb.astype(F32), axis=1)
    lower = jnp.cumsum(lbp, axis=1) - lbp[:, :1]

    cs, sn = _rope_tables(n, lc)
    nch5 = s // S5_CHUNK

    out = None
    for l in range(depth):
        last = l == depth - 1
        need_ctx = not last
        msel = jnp.stack([jnp.broadcast_to(mods[l, bsz][None], (bsz, 6, d)), mods[l, :bsz]], axis=1)

        qa, qb, hg, u, ub = _proj(x_all, msel, norm1_w[l].reshape(1, d).astype(F32), w_in[l].astype(BF16),
                                  cs, sn, lc, tm)

        a_out = _swa(qa, swa_sink[l].astype(F32), lc, need_ctx)
        b_out = _na(qb, _na_bias(na_rpb[l]), lc, need_ctx)
        o_f, o_b = _hgrn(hg, lower[:, l], lc)

        kmat, bmat, cmat, at = _s5_matrices(
            s5_lam_re[l], s5_lam_im[l], s5_log_dt[l], s5_b_re[l], s5_b_im[l], s5_c_re[l], s5_c_im[l])
        ug = jnp.transpose(ub.reshape(bsz, nch5, S5_CHUNK, S5_GROUPS, S5_GROUP_CH), (3, 1, 0, 2, 4))
        yg = _s5(ug.reshape(S5_GROUPS, nch5 * bsz, 256), kmat, bmat, cmat, at, bsz, lc // S5_CHUNK)
        y5 = jnp.transpose(yg.reshape(S5_GROUPS, nch5, bsz, S5_CHUNK, S5_GROUP_CH), (2, 1, 3, 0, 4))
        y5 = y5.reshape(bsz, s, 256)

        rw = jnp.concatenate([moe_group_w[l], moe_expert_w[l],
                              jnp.zeros((d, 128 - MOE_GROUPS - MOE_EXPERTS), F32)], axis=1).astype(F32)
        rb = jnp.concatenate([moe_group_b[l], moe_expert_b[l],
                              jnp.zeros((128 - MOE_GROUPS - MOE_EXPERTS,), F32)]).reshape(1, 128).astype(F32)
        x_mid, h2, route, cnt = _outproj(
            a_out, b_out, o_f, o_b, hg, y5, u, s5_d[l].reshape(1, 256).astype(F32), x_all, msel,
            jnp.tile(hgrn_norm_w[l].astype(F32), 4).reshape(1, 256),
            s5_glu_w[l].astype(BF16), s5_glu_b[l].reshape(1, 256).astype(F32),
            w_out[l].astype(BF16), norm2_w[l].reshape(1, d).astype(F32), rw, rb, lc, tm, need_ctx)

        rpb_rows = x_mid.shape[1]
        route2 = route.reshape(bsz * rpb_rows, 128)
        counts = cnt[0, MOE_GROUPS:MOE_GROUPS + MOE_EXPERTS].astype(jnp.int32)
        slots3, te, nact, n_slots = _dispatch_plan(route2, counts, tm, MOE_TILE)
        xs = _dispatch(h2, slots3, n_slots, tm)
        ys = _moe(xs, moe_w_gate, moe_w_up, moe_w_down, l, te, nact, MOE_TILE)
        res = _combine(x_mid.reshape(bsz * rpb_rows, d), route2, ys, slots3, msel,
                       final_norm_w.reshape(1, d).astype(F32), rpb_rows, lc if need_ctx else 0, tm, last)
        res = res.reshape(bsz, rpb_rows, d)
        if last:
            out = res
        else:
            x_all = res
    return out.astype(x.dtype)
```
